```python
import math
import jax, jax.numpy as jnp
from jax import lax
import numpy as np

D_MODEL = 1024
BATCH = 2
SEQ = 8192
DEPTH = 2

N_META = 16
BLOCK = 128
PAD = BLOCK - N_META

D_RNN = D_MODEL
LRU_BLOCKS = 8
LRU_BS = D_RNN // LRU_BLOCKS
LRU_C = 8.0
CONV_A = 4

N_Q_HEADS = 16
N_KV_HEADS = 2
HEAD_DIM = 64
Q_PER_KV = N_Q_HEADS // N_KV_HEADS
WINDOW = 128
Q_DIM = N_Q_HEADS * HEAD_DIM
KV_DIM = N_KV_HEADS * HEAD_DIM

EVEN_IN = 2 * D_RNN + Q_DIM + 2 * KV_DIM
EVEN_MIX = D_RNN + Q_DIM

D_SSM = 2 * D_MODEL
SSD_HEADDIM = 64
SSD_HEADS = D_SSM // SSD_HEADDIM
SSD_GROUPS = 8
SSD_HPG = SSD_HEADS // SSD_GROUPS
SSD_STATE = 128
CONV_C = 4
SSD_CONV_DIM = D_SSM + 2 * SSD_GROUPS * SSD_STATE
ODD_IN = D_SSM + SSD_CONV_DIM + SSD_HEADS

D_FF = 2816
CONV_F = 3

EPS = 1e-6

kernel_name = "hybrid_rglru_swa_sink_ssd_convffn"


def rms_norm(x, w):
    x32 = x.astype(jnp.float32)
    y = x32 * lax.rsqrt(jnp.mean(x32 * x32, axis=-1, keepdims=True) + EPS)
    return (y * w.astype(jnp.float32)).astype(x.dtype)


def causal_dwconv(x, w, b):
    k = w.shape[0]
    y = lax.conv_general_dilated(
        x, w[:, None, :].astype(x.dtype), window_strides=(1,), padding=[(k - 1, 0)],
        dimension_numbers=("NWC", "WIO", "NWC"), feature_group_count=x.shape[-1])
    return y + b.astype(x.dtype)


def alibi_slopes(n_heads):
    return 2.0 ** (-8.0 * jnp.arange(1, n_heads + 1, dtype=jnp.float32) / n_heads)


def rg_lru(x, w_a, b_a, w_x, b_x, lam):
    bsz, L, _ = x.shape
    x32 = x.astype(jnp.float32)
    xb = x32.reshape(bsz, L, LRU_BLOCKS, LRU_BS)
    r = jax.nn.sigmoid(jnp.einsum("blni,nij->blnj", xb, w_a.astype(jnp.float32)).reshape(bsz, L, D_RNN) + b_a)
    i = jax.nn.sigmoid(jnp.einsum("blni,nij->blnj", xb, w_x.astype(jnp.float32)).reshape(bsz, L, D_RNN) + b_x)
    log_a = -LRU_C * r * jax.nn.softplus(-lam.astype(jnp.float32))
    a = jnp.exp(log_a)
    u = jnp.sqrt(-jnp.expm1(2.0 * log_a)) * (i * x32)

    def combine(c1, c2):
        a1, b1 = c1
        a2, b2 = c2
        return a1 * a2, a2 * b1 + b2

    _, h = lax.associative_scan(combine, (a, u), axis=1)
    return h.astype(x.dtype)


def swa_sink_alibi(q, k, v, sinks):
    bsz, L = q.shape[:2]
    Lp = L + PAD
    nblk = Lp // BLOCK
    q = q.astype(jnp.float32)
    k = k.astype(jnp.float32)
    v = v.astype(jnp.float32)
    padw = ((0, 0), (PAD, 0), (0, 0), (0, 0))
    qb = jnp.pad(q, padw).reshape(bsz, nblk, BLOCK, N_KV_HEADS, Q_PER_KV, HEAD_DIM)
    kb = jnp.pad(k, padw).reshape(bsz, nblk, BLOCK, N_KV_HEADS, HEAD_DIM)
    vb = jnp.pad(v, padw).reshape(bsz, nblk, BLOCK, N_KV_HEADS, HEAD_DIM)
    shift = ((0, 0), (1, 0), (0, 0), (0, 0), (0, 0))
    k_band = jnp.concatenate([jnp.pad(kb, shift)[:, :-1], kb], axis=2)
    v_band = jnp.concatenate([jnp.pad(vb, shift)[:, :-1], vb], axis=2)
    k_meta = k[:, :N_META]
    v_meta = v[:, :N_META]
    scale = HEAD_DIM ** -0.5
    s_band = jnp.einsum("bnqkgd,bnskd->bnkgqs", qb, k_band) * scale
    s_meta = jnp.einsum("bnqkgd,bmkd->bnkgqm", qb, k_meta) * scale

    blk = jnp.arange(nblk)
    t = blk[:, None] * BLOCK + jnp.arange(BLOCK)[None, :] - PAD
    s = (blk[:, None] - 1) * BLOCK + jnp.arange(2 * BLOCK)[None, :] - PAD
    dist_band = t[:, :, None] - s[:, None, :]
    band_ok = (s[:, None, :] >= N_META) & (dist_band >= 0) & (dist_band < WINDOW)
    dist_meta = t[:, :, None] - jnp.arange(N_META)[None, None, :]
    meta_ok = dist_meta >= 0

    slopes = alibi_slopes(N_Q_HEADS).reshape(N_KV_HEADS, Q_PER_KV)[:, :, None, None]
    pen_band = slopes * dist_band[:, None, None].astype(jnp.float32)
    pen_meta = slopes * jnp.minimum(dist_meta, WINDOW)[:, None, None].astype(jnp.float32)
    s_band = jnp.where(band_ok[:, None, None], s_band - pen_band, -jnp.inf)
    s_meta = jnp.where(meta_ok[:, None, None], s_meta - pen_meta, -jnp.inf)

    sink = sinks.astype(jnp.float32).reshape(N_KV_HEADS, Q_PER_KV)[:, :, None, None]
    mx = jnp.maximum(jnp.maximum(s_band.max(-1, keepdims=True), s_meta.max(-1, keepdims=True)), sink)
    p_band = jnp.exp(s_band - mx)
    p_meta = jnp.exp(s_meta - mx)
    denom = p_band.sum(-1, keepdims=True) + p_meta.sum(-1, keepdims=True) + jnp.exp(sink - mx)
    p_band = p_band / denom
    p_meta = p_meta / denom
    o = (jnp.einsum("bnkgqs,bnskd->bnqkgd", p_band, v_band)
         + jnp.einsum("bnkgqm,bmkd->bnqkgd", p_meta, v_meta))
    return o.reshape(bsz, Lp, Q_DIM)[:, PAD:]


def griffin_swa_mixer(u, w_in, conv_w, conv_b, w_a, b_a, w_x, b_x, lam, sinks, w_out):
    bsz, L, _ = u.shape
    proj = u @ w_in
    gate, xr, q, k, v = jnp.split(
        proj, [D_RNN, 2 * D_RNN, 2 * D_RNN + Q_DIM, 2 * D_RNN + Q_DIM + KV_DIM], axis=-1)
    xr = causal_dwconv(xr, conv_w, conv_b)
    y_a = jax.nn.gelu(gate, approximate=True) * rg_lru(xr, w_a, b_a, w_x, b_x, lam)
    y_b = swa_sink_alibi(q.reshape(bsz, L, N_Q_HEADS, HEAD_DIM),
                         k.reshape(bsz, L, N_KV_HEADS, HEAD_DIM),
                         v.reshape(bsz, L, N_KV_HEADS, HEAD_DIM), sinks).astype(u.dtype)
    return jnp.concatenate([y_a, y_b], axis=-1) @ w_out


def ssd_chunked(x, dt, a, b_in, c_in):
    bsz, Lp = x.shape[:2]
    nc = Lp // BLOCK
    x = x.reshape(bsz, nc, BLOCK, SSD_GROUPS, SSD_HPG, SSD_HEADDIM)
    dt = dt.reshape(bsz, nc, BLOCK, SSD_GROUPS, SSD_HPG)
    bc = b_in.reshape(bsz, nc, BLOCK, SSD_GROUPS, SSD_STATE)
    cc = c_in.reshape(bsz, nc, BLOCK, SSD_GROUPS, SSD_STATE)
    cs = jnp.cumsum(dt * a.reshape(SSD_GROUPS, SSD_HPG), axis=2)
    xdt = x * dt[..., None]
    cs_t = jnp.moveaxis(cs, 2, -1)
    seg = cs_t[..., :, None] - cs_t[..., None, :]
    tril = jnp.tril(jnp.ones((BLOCK, BLOCK), dtype=bool))
    decay_in = jnp.exp(jnp.where(tril, seg, -jnp.inf))
    cb = jnp.einsum("bclgn,bcsgn->bcgls", cc, bc)
    y_diag = jnp.einsum("bcgls,bcghls,bcsghp->bclghp", cb, decay_in, xdt)
    cs_last = cs[:, :, -1:]
    chunk_states = jnp.einsum("bclgn,bclgh,bclghp->bcghpn", bc, jnp.exp(cs_last - cs), xdt)
    chunk_decay = jnp.exp(cs_last[:, :, 0])

    def step(state, inp):
        dec, st = inp
        return state * dec[..., None, None] + st, state

    init = jnp.zeros_like(chunk_states[:, 0])
    _, prev = lax.scan(step, init, (jnp.moveaxis(chunk_decay, 1, 0), jnp.moveaxis(chunk_states, 1, 0)))
    prev = jnp.moveaxis(prev, 0, 1)
    y_off = jnp.einsum("bclgn,bcghpn,bclgh->bclghp", cc, prev, jnp.exp(cs))
    return (y_diag + y_off).reshape(bsz, Lp, SSD_HEADS, SSD_HEADDIM)


def mamba2_mixer(u, w_in, conv_w, conv_b, dt_bias, a_log, d_skip, gate_norm, w_out):
    bsz, L, _ = u.shape
    proj = u @ w_in
    z, xbc, dt = jnp.split(proj, [D_SSM, D_SSM + SSD_CONV_DIM], axis=-1)
    xbc = jax.nn.silu(causal_dwconv(xbc, conv_w, conv_b))
    xs, bs, cs = jnp.split(xbc, [D_SSM, D_SSM + SSD_GROUPS * SSD_STATE], axis=-1)
    xs = xs.reshape(bsz, L, SSD_HEADS, SSD_HEADDIM).astype(jnp.float32)
    bs = bs.reshape(bsz, L, SSD_GROUPS, SSD_STATE).astype(jnp.float32)
    cs = cs.reshape(bsz, L, SSD_GROUPS, SSD_STATE).astype(jnp.float32)
    dt = jax.nn.softplus(dt.astype(jnp.float32) + dt_bias.astype(jnp.float32))
    a = -jnp.exp(a_log.astype(jnp.float32))

    def front_pad(t):
        return jnp.pad(t, ((0, 0), (PAD, 0)) + ((0, 0),) * (t.ndim - 2))

    y = ssd_chunked(front_pad(xs), front_pad(dt), a, front_pad(bs), front_pad(cs))[:, PAD:]
    y = y + d_skip.astype(jnp.float32)[:, None] * xs
    y = y.reshape(bsz, L, D_SSM) * jax.nn.silu(z.astype(jnp.float32))
    yg = y.reshape(bsz, L, SSD_GROUPS, D_SSM // SSD_GROUPS)
    yg = yg * lax.rsqrt(jnp.mean(yg * yg, axis=-1, keepdims=True) + EPS)
    y = yg.reshape(bsz, L, D_SSM) * gate_norm.astype(jnp.float32)
    return y.astype(u.dtype) @ w_out


def conv_ffn(u, w_up, conv_w, conv_b, w_down):
    h = causal_dwconv(u @ w_up, conv_w, conv_b)
    g, up = jnp.split(h, 2, axis=-1)
    return (jax.nn.gelu(g, approximate=True) * up) @ w_down


def setup_inputs(seed: int = 0) -> dict:
    key = jax.random.key(seed)
    k = jax.random.split(key, 36)

    def normal(kk, shape, scale):
        return jax.random.normal(kk, shape, jnp.float32) * scale

    def gain(kk, n):
        return 1.0 + 0.05 * jax.random.normal(kk, (n,), jnp.float32)

    u = jax.random.uniform(k[11], (D_RNN,), jnp.float32, 0.9, 0.999)
    a0 = u ** (1.0 / LRU_C)
    lru_lambda = jnp.log(a0) - jnp.log1p(-a0)
    dt0 = jnp.exp(jax.random.uniform(k[25], (SSD_HEADS,), jnp.float32, math.log(1e-3), math.log(1e-1)))
    dt_bias = dt0 + jnp.log(-jnp.expm1(-dt0))
    a_log = jnp.log(jax.random.uniform(k[26], (SSD_HEADS,), jnp.float32, 1.0, 16.0))

    return {
        "x": normal(k[0], (BATCH, SEQ, D_MODEL), 1.0),
        "meta_tokens": normal(k[1], (N_META, D_MODEL), 1.0),
        "l0_mix_pre_norm": gain(k[2], D_MODEL),
        "l0_mix_post_norm": gain(k[3], D_MODEL),
        "l0_w_in": normal(k[4], (D_MODEL, EVEN_IN), D_MODEL ** -0.5),
        "l0_lru_conv_w": normal(k[5], (CONV_A, D_RNN), CONV_A ** -0.5),
        "l0_lru_conv_b": normal(k[6], (D_RNN,), 0.01),
        "l0_lru_w_a": normal(k[7], (LRU_BLOCKS, LRU_BS, LRU_BS), LRU_BS ** -0.5),
        "l0_lru_b_a": normal(k[8], (D_RNN,), 0.01),
        "l0_lru_w_x": normal(k[9], (LRU_BLOCKS, LRU_BS, LRU_BS), LRU_BS ** -0.5),
        "l0_lru_b_x": normal(k[10], (D_RNN,), 0.01),
        "l0_lru_lambda": lru_lambda,
        "l0_attn_sinks": normal(k[12], (N_Q_HEADS,), 0.5),
        "l0_w_out": normal(k[13], (EVEN_MIX, D_MODEL), EVEN_MIX ** -0.5),
        "l0_ffn_pre_norm": gain(k[14], D_MODEL),
        "l0_ffn_post_norm": gain(k[15], D_MODEL),
        "l0_ffn_w_up": normal(k[16], (D_MODEL, 2 * D_FF), D_MODEL ** -0.5),
        "l0_ffn_conv_w": normal(k[17], (CONV_F, 2 * D_FF), CONV_F ** -0.5),
        "l0_ffn_conv_b": normal(k[18], (2 * D_FF,), 0.01),
        "l0_ffn_w_down": normal(k[19], (D_FF, D_MODEL), D_FF ** -0.5),
        "l1_mix_pre_norm": gain(k[20], D_MODEL),
        "l1_mix_post_norm": gain(k[21], D_MODEL),
        "l1_w_in": normal(k[22], (D_MODEL, ODD_IN), D_MODEL ** -0.5),
        "l1_ssm_conv_w": normal(k[23], (CONV_C, SSD_CONV_DIM), CONV_C ** -0.5),
        "l1_ssm_conv_b": normal(k[24], (SSD_CONV_DIM,), 0.01),
        "l1_dt_bias": dt_bias,
        "l1_a_log": a_log,
        "l1_d_skip": 1.0 + 0.1 * jax.random.normal(k[27], (SSD_HEADS,), jnp.float32),
        "l1_gate_norm": gain(k[28], D_SSM),
        "l1_w_out": normal(k[29], (D_SSM, D_MODEL), D_SSM ** -0.5),
        "l1_ffn_pre_norm": gain(k[30], D_MODEL),
        "l1_ffn_post_norm": gain(k[31], D_MODEL),
        "l1_ffn_w_up": normal(k[32], (D_MODEL, 2 * D_FF), D_MODEL ** -0.5),
        "l1_ffn_conv_w": normal(k[33], (CONV_F, 2 * D_FF), CONV_F ** -0.5),
        "l1_ffn_conv_b": normal(k[34], (2 * D_FF,), 0.01),
        "l1_ffn_w_down": normal(k[35], (D_FF, D_MODEL), D_FF ** -0.5),
    }


def reference(x, meta_tokens,
              l0_mix_pre_norm, l0_mix_post_norm, l0_w_in, l0_lru_conv_w, l0_lru_conv_b,
              l0_lru_w_a, l0_lru_b_a, l0_lru_w_x, l0_lru_b_x, l0_lru_lambda, l0_attn_sinks, l0_w_out,
              l0_ffn_pre_norm, l0_ffn_post_norm, l0_ffn_w_up, l0_ffn_conv_w, l0_ffn_conv_b, l0_ffn_w_down,
              l1_mix_pre_norm, l1_mix_post_norm, l1_w_in, l1_ssm_conv_w, l1_ssm_conv_b,
              l1_dt_bias, l1_a_log, l1_d_skip, l1_gate_norm, l1_w_out,
              l1_ffn_pre_norm, l1_ffn_post_norm, l1_ffn_w_up, l1_ffn_conv_w, l1_ffn_conv_b, l1_ffn_w_down):
    bsz = x.shape[0]
    meta = jnp.broadcast_to(meta_tokens.astype(x.dtype)[None], (bsz, N_META, D_MODEL))
    h = jnp.concatenate([meta, x], axis=1)
    layers = [
        (l0_mix_pre_norm, l0_mix_post_norm,
         (l0_w_in, l0_lru_conv_w, l0_lru_conv_b, l0_lru_w_a, l0_lru_b_a, l0_lru_w_x, l0_lru_b_x,
          l0_lru_lambda, l0_attn_sinks, l0_w_out),
         (l0_ffn_pre_norm, l0_ffn_post_norm, l0_ffn_w_up, l0_ffn_conv_w, l0_ffn_conv_b, l0_ffn_w_down)),
        (l1_mix_pre_norm, l1_mix_post_norm,
         (l1_w_in, l1_ssm_conv_w, l1_ssm_conv_b, l1_dt_bias, l1_a_log, l1_d_skip, l1_gate_norm, l1_w_out),
         (l1_ffn_pre_norm, l1_ffn_post_norm, l1_ffn_w_up, l1_ffn_conv_w, l1_ffn_conv_b, l1_ffn_w_down)),
    ]
    for i in range(DEPTH):
        pre, post, mix, ffn = layers[i]
        mixer = griffin_swa_mixer if i % 2 == 0 else mamba2_mixer
        h = h + rms_norm(mixer(rms_norm(h, pre), *mix), post)
        f_pre, f_post, w_up, c_w, c_b, w_down = ffn
        h = h + rms_norm(conv_ffn(rms_norm(h, f_pre), w_up, c_w, c_b, w_down), f_post)
    return h[:, N_META:]
```

```python
import functools
import math

import jax
import jax.numpy as jnp
from jax import lax
from jax.experimental import pallas as pl
from jax.experimental.pallas import tpu as pltpu

D_MODEL = 1024
BATCH = 2
SEQ = 8192
N_META = 16
BLOCK = 128
PAD = BLOCK - N_META
LP = PAD + N_META + SEQ
NBLK = LP // BLOCK
ROWS = BATCH * LP

D_RNN = D_MODEL
LRU_BLOCKS = 8
LRU_BS = D_RNN // LRU_BLOCKS
LRU_C = 8.0
CONV_A = 4

N_Q_HEADS = 16
N_KV_HEADS = 2
HEAD_DIM = 64
Q_PER_KV = N_Q_HEADS // N_KV_HEADS
WINDOW = 128
Q_DIM = N_Q_HEADS * HEAD_DIM
KV_DIM = N_KV_HEADS * HEAD_DIM

D_SSM = 2 * D_MODEL
SSD_HEADDIM = 64
SSD_HEADS = D_SSM // SSD_HEADDIM
SSD_GROUPS = 8
SSD_HPG = SSD_HEADS // SSD_GROUPS
SSD_STATE = 128
CONV_C = 4
SSD_BC = SSD_GROUPS * SSD_STATE
SSD_CONV_DIM = D_SSM + 2 * SSD_BC

D_FF = 2816
CONV_F = 3
FF_CHUNK = 256
N_FF_CHUNKS = D_FF // FF_CHUNK

EPS = 1e-6
LANES = 128
SUBLANES = 8

VMEM_LIMIT = 56 * 1024 * 1024

F32 = jnp.float32
BF16 = jnp.bfloat16


def _rms(x, w):
    return x * lax.rsqrt(jnp.mean(x * x, axis=-1, keepdims=True) + EPS) * w


def _softplus(y):
    return jnp.maximum(y, 0.0) + jnp.log1p(jnp.exp(-jnp.abs(y)))


def _shift_rows(x, d, fill):
    n, c = x.shape
    if d % SUBLANES == 0:
        return jnp.concatenate([jnp.full((d, c), fill, x.dtype), x[: n - d]], axis=0)
    row = lax.broadcasted_iota(jnp.int32, x.shape, 0)
    return jnp.where(row >= d, pltpu.roll(x, d, 0), fill)


def _with_history(carry, x, k):
    xx = jnp.concatenate([carry, x], axis=0)
    return pltpu.roll(xx, k, 0)[SUBLANES:]


def _row_valid(tile_in_batch, tm):
    row = tile_in_batch * tm + lax.broadcasted_iota(jnp.int32, (tm, 1), 0)
    return row >= PAD


def _const_spec(shape):
    nd = len(shape)
    return pl.BlockSpec(shape, lambda i: (0,) * nd, pipeline_mode=pl.Buffered(1))


def _params():
    return pltpu.CompilerParams(dimension_semantics=("arbitrary",), vmem_limit_bytes=VMEM_LIMIT)


def _norm_matmul_kernel(h_ref, nw_ref, w_ref, *out_refs, splits):
    u = _rms(h_ref[...], nw_ref[...]).astype(BF16)
    c0 = 0
    for o_ref, width in zip(out_refs, splits):
        o_ref[...] = jnp.dot(u, w_ref[:, c0:c0 + width], preferred_element_type=F32).astype(o_ref.dtype)
        c0 += width


def _norm_matmul(h, norm_w, w, splits, dtypes, tm, name):
    n_cols = w.shape[1]
    return pl.pallas_call(
        functools.partial(_norm_matmul_kernel, splits=splits),
        grid=(ROWS // tm,),
        in_specs=[pl.BlockSpec((tm, D_MODEL), lambda i: (i, 0)),
                  _const_spec((1, D_MODEL)),
                  _const_spec((D_MODEL, n_cols))],
        out_specs=[pl.BlockSpec((tm, s), lambda i: (i, 0)) for s in splits],
        out_shape=[jax.ShapeDtypeStruct((ROWS, s), dt) for s, dt in zip(splits, dtypes)],
        compiler_params=_params(),
        name=name,
    )(h, norm_w.reshape(1, D_MODEL), w)


def _outproj_kernel(*refs, widths, tm, tpb):
    a_refs = refs[:len(widths)]
    w_ref, h_ref, pw_ref, o_ref = refs[len(widths):]
    y = None
    r0 = 0
    for a_ref, width in zip(a_refs, widths):
        part = jnp.dot(a_ref[...], w_ref[r0:r0 + width, :], preferred_element_type=F32)
        y = part if y is None else y + part
        r0 += width
    o = h_ref[...] + _rms(y, pw_ref[...])
    o_ref[...] = jnp.where(_row_valid(pl.program_id(0) % tpb, tm), o, 0.0)


def _outproj(acts, w, h, post_w, tm, name):
    widths = tuple(a.shape[1] for a in acts)
    return pl.pallas_call(
        functools.partial(_outproj_kernel, widths=widths, tm=tm, tpb=LP // tm),
        grid=(ROWS // tm,),
        in_specs=[pl.BlockSpec((tm, k), lambda i: (i, 0)) for k in widths]
                 + [_const_spec((sum(widths), D_MODEL)),
                    pl.BlockSpec((tm, D_MODEL), lambda i: (i, 0)),
                    _const_spec((1, D_MODEL))],
        out_specs=pl.BlockSpec((tm, D_MODEL), lambda i: (i, 0)),
        out_shape=jax.ShapeDtypeStruct((ROWS, D_MODEL), F32),
        compiler_params=_params(),
        name=name,
    )(*acts, w, h, post_w.reshape(1, D_MODEL))


def _ffn_kernel(h_ref, nw_ref, wup_ref, cw_ref, cb_ref, wdn_ref, pw_ref, o_ref, carry_ref, acc_ref, *, tm, tpb):
    tile = pl.program_id(0) % tpb

    @pl.when(tile == 0)
    def _():
        carry_ref[...] = jnp.zeros_like(carry_ref)

    h = h_ref[...]
    u = _rms(h, nw_ref[...]).astype(BF16)
    acc_ref[...] = jnp.zeros_like(acc_ref)

    def conv_chunk(j):
        x = jnp.dot(u, wup_ref[j], preferred_element_type=F32)
        carry = carry_ref[j]
        x1 = _with_history(carry, x, 1)
        x2 = _with_history(carry, x, 2)
        carry_ref[j] = x[tm - SUBLANES:]
        w = cw_ref[j]
        return cb_ref[j] + w[2:3] * x + w[1:2] * x1 + w[0:1] * x2

    def body(j, carry):
        g = conv_chunk(j)
        v = conv_chunk(j + N_FF_CHUNKS)
        a = (jax.nn.gelu(g, approximate=True) * v).astype(BF16)
        acc_ref[...] += jnp.dot(a, wdn_ref[j], preferred_element_type=F32)
        return carry

    lax.fori_loop(0, N_FF_CHUNKS, body, 0)
    o = h + _rms(acc_ref[...], pw_ref[...])
    o_ref[...] = jnp.where(_row_valid(tile, tm), o, 0.0)


def _ffn(h, pre_w, w_up, conv_w, conv_b, w_down, post_w, tm, name):
    nch = 2 * N_FF_CHUNKS
    wup = w_up.reshape(D_MODEL, nch, FF_CHUNK).transpose(1, 0, 2).astype(BF16)
    cw = conv_w.reshape(CONV_F, nch, FF_CHUNK).transpose(1, 0, 2)
    cb = conv_b.reshape(nch, 1, FF_CHUNK)
    wdn = w_down.reshape(N_FF_CHUNKS, FF_CHUNK, D_MODEL).astype(BF16)
    return pl.pallas_call(
        functools.partial(_ffn_kernel, tm=tm, tpb=LP // tm),
        grid=(ROWS // tm,),
        in_specs=[pl.BlockSpec((tm, D_MODEL), lambda i: (i, 0)),
                  _const_spec((1, D_MODEL)),
                  _const_spec((nch, D_MODEL, FF_CHUNK)),
                  _const_spec((nch, CONV_F, FF_CHUNK)),
                  _const_spec((nch, 1, FF_CHUNK)),
                  _const_spec((N_FF_CHUNKS, FF_CHUNK, D_MODEL)),
                  _const_spec((1, D_MODEL))],
        out_specs=pl.BlockSpec((tm, D_MODEL), lambda i: (i, 0)),
        out_shape=jax.ShapeDtypeStruct((ROWS, D_MODEL), F32),
        scratch_shapes=[pltpu.VMEM((nch, SUBLANES, FF_CHUNK), F32),
                        pltpu.VMEM((tm, D_MODEL), F32)],
        compiler_params=_params(),
        name=name,
    )(h, pre_w.reshape(1, D_MODEL), wup, cw, cb, wdn, post_w.reshape(1, D_MODEL))


def _linear_scan(a, u):
    n = a.shape[0]
    d = 1
    while d < n:
        a_s = _shift_rows(a, d, 1.0)
        u_s = _shift_rows(u, d, 0.0)
        u = a * u_s + u
        a = a * a_s
        d *= 2
    return a, u


def _lru_kernel(gate_ref, xr_ref, cw_ref, cb_ref, wg_ref, ba_ref, bx_ref, lam_ref, o_ref, cx_ref, ch_ref, *, tr):
    tile = pl.program_id(0) % (LP // tr)

    @pl.when(tile == 0)
    def _():
        cx_ref[...] = jnp.zeros_like(cx_ref)
        ch_ref[...] = jnp.zeros_like(ch_ref)

    valid = _row_valid(tile, tr)
    for n in range(LRU_BLOCKS):
        cols = slice(n * LRU_BS, (n + 1) * LRU_BS)
        x = xr_ref[:, cols]
        carry = cx_ref[:, cols]
        cw = cw_ref[:, cols]
        xc = cb_ref[:, cols] + cw[3:4] * x
        for k in range(1, CONV_A):
            xc = xc + cw[CONV_A - 1 - k:CONV_A - k] * _with_history(carry, x, k)
        cx_ref[:, cols] = x[tr - SUBLANES:]

        pre = jnp.dot(xc.astype(BF16), wg_ref[n], preferred_element_type=F32)
        r = jax.nn.sigmoid(pre[:, :LRU_BS] + ba_ref[:, cols])
        i = jax.nn.sigmoid(pre[:, LRU_BS:] + bx_ref[:, cols])
        log_a = (-LRU_C * _softplus(-lam_ref[:, cols])) * r
        a = jnp.exp(log_a)
        u = jnp.sqrt(1.0 - a * a) * (i * xc)
        u = jnp.where(valid, u, 0.0)
        a_cum, hloc = _linear_scan(a, u)
        h = hloc + a_cum * ch_ref[SUBLANES - 1:SUBLANES, cols]
        ch_ref[:, cols] = h[tr - SUBLANES:]
        o_ref[:, cols] = (jax.nn.gelu(gate_ref[:, cols], approximate=True) * h).astype(o_ref.dtype)


def _lru(gx, conv_w, conv_b, w_a, b_a, w_x, b_x, lam, tr):
    wg = jnp.concatenate([w_a, w_x], axis=-1).astype(BF16)
    row = lambda v: v.reshape(1, D_RNN)
    return pl.pallas_call(
        functools.partial(_lru_kernel, tr=tr),
        grid=(ROWS // tr,),
        in_specs=[pl.BlockSpec((tr, D_RNN), lambda i: (i, 0)),
                  pl.BlockSpec((tr, D_RNN), lambda i: (i, 1)),
                  _const_spec((CONV_A, D_RNN)),
                  _const_spec((1, D_RNN)),
                  _const_spec((LRU_BLOCKS, LRU_BS, 2 * LRU_BS)),
                  _const_spec((1, D_RNN)),
                  _const_spec((1, D_RNN)),
                  _const_spec((1, D_RNN))],
        out_specs=pl.BlockSpec((tr, D_RNN), lambda i: (i, 0)),
        out_shape=jax.ShapeDtypeStruct((ROWS, D_RNN), BF16),
        scratch_shapes=[pltpu.VMEM((SUBLANES, D_RNN), F32),
                        pltpu.VMEM((SUBLANES, D_RNN), F32)],
        compiler_params=_params(),
        name="l0_lru",
    )(gx, gx, conv_w, row(conv_b), wg, row(b_a), row(b_x), row(lam))


def _attn_kernel(sink_ref, q_ref, km_ref, kp_ref, kc_ref, vm_ref, vp_ref, vc_ref, o_ref):
    n = pl.program_id(0) % NBLK
    qrow = lax.broadcasted_iota(jnp.int32, (BLOCK, BLOCK), 0)
    j = lax.broadcasted_iota(jnp.int32, (BLOCK, BLOCK), 1)
    d_meta = n * BLOCK + qrow - j
    neg_meta = jnp.where((j >= PAD) & (d_meta >= 0), -jnp.minimum(d_meta, WINDOW).astype(F32), -jnp.inf)
    d_prev = BLOCK + qrow - j
    neg_prev = jnp.where((d_prev < WINDOW) & (n >= 2), -d_prev.astype(F32), -jnp.inf)
    d_cur = qrow - j
    neg_cur = jnp.where((d_cur >= 0) & (n >= 1), -d_cur.astype(F32), -jnp.inf)
    neg_dist = jnp.concatenate([neg_meta, neg_prev, neg_cur], axis=1)

    lane = lax.broadcasted_iota(jnp.int32, (1, LANES), 1)
    for g in range(N_KV_HEADS):
        gcols = slice(g * LANES, (g + 1) * LANES)
        k_all = jnp.concatenate([km_ref[:, gcols], kp_ref[:, gcols], kc_ref[:, gcols]], axis=0)
        v_all = jnp.concatenate([vm_ref[:, gcols], vp_ref[:, gcols], vc_ref[:, gcols]], axis=0)
        for pair in range(Q_PER_KV // 2):
            pcols = slice((g * (Q_PER_KV // 2) + pair) * LANES, (g * (Q_PER_KV // 2) + pair + 1) * LANES)
            qp = q_ref[:, pcols]
            o_pair = jnp.zeros((BLOCK, LANES), F32)
            for half in range(2):
                head = g * Q_PER_KV + 2 * pair + half
                in_half = (lane >= half * HEAD_DIM) & (lane < (half + 1) * HEAD_DIM)
                qh = jnp.where(in_half, qp, 0.0).astype(BF16) * (HEAD_DIM ** -0.5)
                s = lax.dot_general(qh.astype(BF16), k_all, (((1,), (1,)), ((), ())), preferred_element_type=F32)
                slope = 2.0 ** (-8.0 * (head + 1) / N_Q_HEADS)
                s = s + slope * neg_dist
                sink = sink_ref[head]
                mx = jnp.maximum(jnp.max(s, axis=-1, keepdims=True), sink)
                e = jnp.exp(s - mx)
                denom = jnp.sum(e, axis=-1, keepdims=True) + jnp.exp(sink - mx)
                vh = jnp.where(in_half, v_all, 0.0).astype(BF16)
                o_pair = o_pair + jnp.dot(e.astype(BF16), vh, preferred_element_type=F32) / denom
            o_ref[:, pcols] = o_pair.astype(o_ref.dtype)


def _attention(qkv, sinks):
    kcol = Q_DIM // (2 * LANES)
    vcol = kcol + 1
    first = lambda i: (i // NBLK) * NBLK
    prev = lambda i: jnp.maximum(i - 1, 0)
    kv = lambda rowf, col: pl.BlockSpec((BLOCK, 2 * LANES), lambda i: (rowf(i), col))
    return pl.pallas_call(
        _attn_kernel,
        grid=(ROWS // BLOCK,),
        in_specs=[pl.BlockSpec(memory_space=pltpu.SMEM),
                  pl.BlockSpec((BLOCK, Q_DIM), lambda i: (i, 0)),
                  kv(first, kcol), kv(prev, kcol), kv(lambda i: i, kcol),
                  kv(first, vcol), kv(prev, vcol), kv(lambda i: i, vcol)],
        out_specs=pl.BlockSpec((BLOCK, Q_DIM), lambda i: (i, 0)),
        out_shape=jax.ShapeDtypeStruct((ROWS, Q_DIM), BF16),
        compiler_params=_params(),
        name="l0_attn",
    )(sinks, qkv, qkv, qkv, qkv, qkv, qkv, qkv)


def _ssd_kernel(z_ref, xbc_ref, dt_ref, cw_ref, cb_ref, dtb_ref, alog_ref, dskip_ref, gn_ref, o_ref,
                carry_ref, state_ref, xc_ref):
    n = pl.program_id(0) % NBLK

    @pl.when(n == 0)
    def _():
        carry_ref[...] = jnp.zeros_like(carry_ref)
        state_ref[...] = jnp.zeros_like(state_ref)

    valid = _row_valid(n, BLOCK)

    cchunk = 4 * LANES
    for c in range(SSD_CONV_DIM // cchunk):
        cols = slice(c * cchunk, (c + 1) * cchunk)
        x = xbc_ref[:, cols]
        carry = carry_ref[:, cols]
        cw = cw_ref[:, cols]
        y = cb_ref[:, cols] + cw[3:4] * x
        for k in range(1, CONV_C):
            y = y + cw[CONV_C - 1 - k:CONV_C - k] * _with_history(carry, x, k)
        carry_ref[:, cols] = x[BLOCK - SUBLANES:]
        xc_ref[:, cols] = jnp.where(valid, jax.nn.silu(y), 0.0)

    dt = jnp.where(valid, _softplus(dt_ref[...] + dtb_ref[...]), 0.0)
    a = -jnp.exp(alog_ref[...])
    cs = dt * a
    d = 1
    while d < BLOCK:
        cs = cs + _shift_rows(cs, d, 0.0)
        d *= 2
    cs_last = cs[BLOCK - 1:BLOCK, :]
    dtw = dt * jnp.exp(cs_last - cs)
    chunk_decay = jnp.exp(cs_last)
    cs_t = cs.T
    dt_t = dt.T
    dtw_t = dtw.T

    row = lax.broadcasted_iota(jnp.int32, (BLOCK, BLOCK), 0)
    col = lax.broadcasted_iota(jnp.int32, (BLOCK, BLOCK), 1)
    tril = row >= col
    lane = lax.broadcasted_iota(jnp.int32, (1, LANES), 1)
    left = lane < SSD_HEADDIM

    gw = SSD_HPG * SSD_HEADDIM
    for g in range(SSD_GROUPS):
        b_g = xc_ref[:, D_SSM + g * SSD_STATE:D_SSM + (g + 1) * SSD_STATE]
        c_g = xc_ref[:, D_SSM + SSD_BC + g * SSD_STATE:D_SSM + SSD_BC + (g + 1) * SSD_STATE].astype(BF16)
        b_t = b_g.T
        cb = jnp.dot(c_g, b_t.astype(BF16), preferred_element_type=F32)
        y_off = jnp.dot(c_g, state_ref[g].astype(BF16), preferred_element_type=F32)
        y_parts = []
        for pair in range(SSD_HPG // 2):
            h0 = g * SSD_HPG + 2 * pair
            pcols = slice(g * gw + pair * LANES, g * gw + (pair + 1) * LANES)
            xp = xc_ref[:, pcols]
            lhs_top, lhs_bot, e_cs = [], [], []
            for h in (h0, h0 + 1):
                cs_col = jnp.broadcast_to(cs[:, h:h + 1], (BLOCK, BLOCK))
                seg = cs_col - cs_t[h:h + 1, :]
                decay = jnp.exp(jnp.where(tril, seg, -jnp.inf))
                lhs_top.append((cb * decay * dt_t[h:h + 1, :]).astype(BF16))
                lhs_bot.append((b_t * dtw_t[h:h + 1, :]).astype(BF16))
                e_cs.append(jnp.exp(cs_col))
            lhs = jnp.concatenate([jnp.concatenate(lhs_top, axis=1), jnp.concatenate(lhs_bot, axis=1)], axis=0)
            x_bd = jnp.concatenate([jnp.where(left, xp, 0.0), jnp.where(left, 0.0, xp)], axis=0).astype(BF16)
            res = jnp.dot(lhs, x_bd, preferred_element_type=F32)
            scols = slice(pair * LANES, (pair + 1) * LANES)
            y_pair = res[:BLOCK] + y_off[:, scols] * jnp.where(left, e_cs[0], e_cs[1])
            dec = jnp.where(left, chunk_decay[:, h0:h0 + 1], chunk_decay[:, h0 + 1:h0 + 2])
            state_ref[g, :, scols] = state_ref[g, :, scols] * dec + res[BLOCK:]
            y_parts.append(y_pair + dskip_ref[:, pcols] * xp)
        gcols = slice(g * gw, (g + 1) * gw)
        y_g = jnp.concatenate(y_parts, axis=1) * jax.nn.silu(z_ref[:, gcols])
        y_g = y_g * lax.rsqrt(jnp.mean(y_g * y_g, axis=-1, keepdims=True) + EPS)
        o_ref[:, gcols] = (y_g * gn_ref[:, gcols]).astype(o_ref.dtype)


def _ssd(z, xbc, dt, conv_w, conv_b, dt_bias, a_log, d_skip, gate_norm):
    pad_lanes = lambda v: jnp.pad(v, (0, LANES - SSD_HEADS)).reshape(1, LANES)
    rowblk = lambda w: pl.BlockSpec((BLOCK, w), lambda i: (i, 0))
    gw = SSD_HPG * SSD_HEADDIM
    return pl.pallas_call(
        _ssd_kernel,
        grid=(ROWS // BLOCK,),
        in_specs=[rowblk(D_SSM), rowblk(SSD_CONV_DIM), rowblk(LANES),
                  _const_spec((CONV_C, SSD_CONV_DIM)),
                  _const_spec((1, SSD_CONV_DIM)),
                  _const_spec((1, LANES)),
                  _const_spec((1, LANES)),
                  _const_spec((1, D_SSM)),
                  _const_spec((1, D_SSM))],
        out_specs=rowblk(D_SSM),
        out_shape=jax.ShapeDtypeStruct((ROWS, D_SSM), BF16),
        scratch_shapes=[pltpu.VMEM((SUBLANES, SSD_CONV_DIM), F32),
                        pltpu.VMEM((SSD_GROUPS, SSD_STATE, gw), F32),
                        pltpu.VMEM((BLOCK, SSD_CONV_DIM), F32)],
        compiler_params=_params(),
        name="l1_ssd",
    )(z, xbc, dt, conv_w, conv_b.reshape(1, SSD_CONV_DIM), pad_lanes(dt_bias), pad_lanes(a_log),
      jnp.repeat(d_skip, SSD_HEADDIM).reshape(1, D_SSM), gate_norm.reshape(1, D_SSM))


def kernel(x, meta_tokens,
           l0_mix_pre_norm, l0_mix_post_norm, l0_w_in, l0_lru_conv_w, l0_lru_conv_b,
           l0_lru_w_a, l0_lru_b_a, l0_lru_w_x, l0_lru_b_x, l0_lru_lambda, l0_attn_sinks, l0_w_out,
           l0_ffn_pre_norm, l0_ffn_post_norm, l0_ffn_w_up, l0_ffn_conv_w, l0_ffn_conv_b, l0_ffn_w_down,
           l1_mix_pre_norm, l1_mix_post_norm, l1_w_in, l1_ssm_conv_w, l1_ssm_conv_b,
           l1_dt_bias, l1_a_log, l1_d_skip, l1_gate_norm, l1_w_out,
           l1_ffn_pre_norm, l1_ffn_post_norm, l1_ffn_w_up, l1_ffn_conv_w, l1_ffn_conv_b, l1_ffn_w_down):
    bsz = x.shape[0]
    meta = jnp.broadcast_to(meta_tokens.astype(x.dtype)[None], (bsz, N_META, D_MODEL))
    h = jnp.concatenate([jnp.zeros((bsz, PAD, D_MODEL), x.dtype), meta, x], axis=1).reshape(ROWS, D_MODEL)

    kq = 2 * D_RNN + Q_DIM
    dup = lambda w: jnp.concatenate([w[:, :HEAD_DIM]] * 2 + [w[:, HEAD_DIM:]] * 2, axis=1)
    w0 = jnp.concatenate([l0_w_in[:, :kq], dup(l0_w_in[:, kq:kq + KV_DIM]), dup(l0_w_in[:, kq + KV_DIM:])],
                         axis=1).astype(BF16)
    gx, qkv = _norm_matmul(h, l0_mix_pre_norm, w0, (2 * D_RNN, Q_DIM + 4 * KV_DIM), (F32, BF16), 640, "l0_inproj")
    y_a = _lru(gx, l0_lru_conv_w, l0_lru_conv_b, l0_lru_w_a, l0_lru_b_a, l0_lru_w_x, l0_lru_b_x, l0_lru_lambda, BLOCK)
    y_b = _attention(qkv, l0_attn_sinks)
    h = _outproj([y_a, y_b], l0_w_out.astype(BF16), h, l0_mix_post_norm, 640, "l0_outproj")
    h = _ffn(h, l0_ffn_pre_norm, l0_ffn_w_up, l0_ffn_conv_w, l0_ffn_conv_b, l0_ffn_w_down, l0_ffn_post_norm,
             640, "l0_ffn")

    w1 = jnp.pad(l1_w_in, ((0, 0), (0, LANES - SSD_HEADS))).astype(BF16)
    z, xbc, dt = _norm_matmul(h, l1_mix_pre_norm, w1, (D_SSM, SSD_CONV_DIM, LANES), (F32, F32, F32), 320, "l1_inproj")
    y = _ssd(z, xbc, dt, l1_ssm_conv_w, l1_ssm_conv_b, l1_dt_bias, l1_a_log, l1_d_skip, l1_gate_norm)
    h = _outproj([y], l1_w_out.astype(BF16), h, l1_mix_post_norm, 640, "l1_outproj")
    h = _ffn(h, l1_ffn_pre_norm, l1_ffn_w_up, l1_ffn_conv_w, l1_ffn_conv_b, l1_ffn_w_down, l1_ffn_post_norm,
             640, "l1_ffn")
    return h.reshape(bsz, LP, D_MODEL)[:, BLOCK:]
```

```python
import functools
import math

import jax
import jax.numpy as jnp
from jax import lax
from jax.experimental import pallas as pl
from jax.experimental.pallas import tpu as pltpu

D_MODEL = 1024
BATCH = 2
SEQ = 8192
N_META = 16
BLOCK = 128
PAD = BLOCK - N_META
LP = PAD + N_META + SEQ
NBLK = LP // BLOCK
ROWS = BATCH * LP

D_RNN = D_MODEL
LRU_BLOCKS = 8
LRU_BS = D_RNN // LRU_BLOCKS
LRU_C = 8.0
CONV_A = 4

N_Q_HEADS = 16
N_KV_HEADS = 2
HEAD_DIM = 64
Q_PER_KV = N_Q_HEADS // N_KV_HEADS
WINDOW = 128
Q_DIM = N_Q_HEADS * HEAD_DIM
KV_DIM = N_KV_HEADS * HEAD_DIM

D_SSM = 2 * D_MODEL
SSD_HEADDIM = 64
SSD_HEADS = D_SSM // SSD_HEADDIM
SSD_GROUPS = 8
SSD_HPG = SSD_HEADS // SSD_GROUPS
SSD_STATE = 128
CONV_C = 4
SSD_BC = SSD_GROUPS * SSD_STATE
SSD_CONV_DIM = D_SSM + 2 * SSD_BC

D_FF = 2816
CONV_F = 3
FF_CHUNK = 256
N_FF_CHUNKS = D_FF // FF_CHUNK

EPS = 1e-6
LANES = 128
SUBLANES = 8

VMEM_LIMIT = 56 * 1024 * 1024

F32 = jnp.float32
BF16 = jnp.bfloat16


def _rms(x, w):
    return x * lax.rsqrt(jnp.mean(x * x, axis=-1, keepdims=True) + EPS) * w


def _softplus(y):
    return jnp.maximum(y, 0.0) + jnp.log1p(jnp.exp(-jnp.abs(y)))


def _shift_rows(x, d, fill):
    n, c = x.shape
    if d % SUBLANES == 0:
        return jnp.concatenate([jnp.full((d, c), fill, x.dtype), x[: n - d]], axis=0)
    row = lax.broadcasted_iota(jnp.int32, x.shape, 0)
    return jnp.where(row >= d, pltpu.roll(x, d, 0), fill)


def _with_history(carry, x, k):
    xx = jnp.concatenate([carry, x], axis=0)
    return pltpu.roll(xx, k, 0)[SUBLANES:]


def _row_valid(tile_in_batch, tm):
    row = tile_in_batch * tm + lax.broadcasted_iota(jnp.int32, (tm, 1), 0)
    return row >= PAD


def _const_spec(shape):
    nd = len(shape)
    return pl.BlockSpec(shape, lambda i: (0,) * nd, pipeline_mode=pl.Buffered(1))


def _params():
    return pltpu.CompilerParams(dimension_semantics=("arbitrary",), vmem_limit_bytes=VMEM_LIMIT)


def _causal_conv(x, carry, cw, cb):
    taps = cw.shape[0]
    y = cb + cw[taps - 1:taps] * x
    for k in range(1, taps):
        y = y + cw[taps - 1 - k:taps - k] * _with_history(carry, x, k)
    return y


L0_IN_COLS = 2 * D_RNN + Q_DIM + 4 * KV_DIM
LRU_CHUNK = 2 * LRU_BS


def _l0_inproj_kernel(h_ref, nw_ref, w_ref, cw_ref, cb_ref, wg_ref, ba_ref, bx_ref, lam_ref,
                      gg_ref, a_ref, u_ref, qkv_ref, cx_ref, *, tm, tpb):
    tile = pl.program_id(0) % tpb

    @pl.when(tile == 0)
    def _():
        cx_ref[...] = jnp.zeros_like(cx_ref)

    u_in = _rms(h_ref[...], nw_ref[...]).astype(BF16)
    valid = _row_valid(tile, tm)
    xs = [jnp.dot(u_in, w_ref[:, D_RNN + c * LRU_CHUNK:D_RNN + (c + 1) * LRU_CHUNK], preferred_element_type=F32)
          for c in range(D_RNN // LRU_CHUNK)]
    gate = jnp.dot(u_in, w_ref[:, :D_RNN], preferred_element_type=F32)
    xcs = []
    for c, x in enumerate(xs):
        cols = slice(c * LRU_CHUNK, (c + 1) * LRU_CHUNK)
        xcs.append(_causal_conv(x, cx_ref[:, cols], cw_ref[:, cols], cb_ref[:, cols]))
        cx_ref[:, cols] = x[tm - SUBLANES:]
    pres = [jnp.dot(xcs[n // 2][:, (n % 2) * LRU_BS:(n % 2 + 1) * LRU_BS].astype(BF16), wg_ref[n],
                    preferred_element_type=F32) for n in range(LRU_BLOCKS)]
    qkv_ref[...] = jnp.dot(u_in, w_ref[:, 2 * D_RNN:], preferred_element_type=F32).astype(qkv_ref.dtype)
    for n in range(LRU_BLOCKS):
        cols = slice(n * LRU_BS, (n + 1) * LRU_BS)
        xc = xcs[n // 2][:, (n % 2) * LRU_BS:(n % 2 + 1) * LRU_BS]
        r = jax.nn.sigmoid(pres[n][:, :LRU_BS] + ba_ref[:, cols])
        i = jax.nn.sigmoid(pres[n][:, LRU_BS:] + bx_ref[:, cols])
        log_a = (-LRU_C * _softplus(-lam_ref[:, cols])) * r
        a = jnp.exp(log_a)
        a_ref[:, cols] = a
        u_ref[:, cols] = jnp.where(valid, jnp.sqrt(1.0 - a * a) * (i * xc), 0.0)
    gg_ref[...] = jax.nn.gelu(gate, approximate=True)


def _l0_inproj(h, norm_w, w, conv_w, conv_b, w_a, b_a, w_x, b_x, lam, tm):
    wg = jnp.concatenate([w_a, w_x], axis=-1).astype(BF16)
    row = lambda v: v.reshape(1, D_RNN)
    rowblk = lambda width: pl.BlockSpec((tm, width), lambda i: (i, 0))
    return pl.pallas_call(
        functools.partial(_l0_inproj_kernel, tm=tm, tpb=LP // tm),
        grid=(ROWS // tm,),
        in_specs=[rowblk(D_MODEL),
                  _const_spec((1, D_MODEL)),
                  _const_spec((D_MODEL, L0_IN_COLS)),
                  _const_spec((CONV_A, D_RNN)),
                  _const_spec((1, D_RNN)),
                  _const_spec((LRU_BLOCKS, LRU_BS, 2 * LRU_BS)),
                  _const_spec((1, D_RNN)),
                  _const_spec((1, D_RNN)),
                  _const_spec((1, D_RNN))],
        out_specs=[rowblk(D_RNN), rowblk(D_RNN), rowblk(D_RNN), rowblk(Q_DIM + 4 * KV_DIM)],
        out_shape=[jax.ShapeDtypeStruct((ROWS, D_RNN), F32)] * 3
                  + [jax.ShapeDtypeStruct((ROWS, Q_DIM + 4 * KV_DIM), BF16)],
        scratch_shapes=[pltpu.VMEM((SUBLANES, D_RNN), F32)],
        compiler_params=_params(),
        name="l0_inproj",
    )(h, norm_w.reshape(1, D_MODEL), w, conv_w, row(conv_b), wg, row(b_a), row(b_x), row(lam))


L1_IN_COLS = D_SSM + SSD_CONV_DIM + LANES
SSD_CONV_CHUNK = 4 * LANES


def _l1_inproj_kernel(h_ref, nw_ref, w_ref, cw_ref, cb_ref, dtb_ref, zs_ref, xc_ref, dt_ref, cx_ref, *, tm, tpb):
    tile = pl.program_id(0) % tpb

    @pl.when(tile == 0)
    def _():
        cx_ref[...] = jnp.zeros_like(cx_ref)

    u_in = _rms(h_ref[...], nw_ref[...]).astype(BF16)
    valid = _row_valid(tile, tm)
    n_chunks = SSD_CONV_DIM // SSD_CONV_CHUNK
    zw = D_SSM // n_chunks
    for c in range(n_chunks):
        cols = slice(c * SSD_CONV_CHUNK, (c + 1) * SSD_CONV_CHUNK)
        x = jnp.dot(u_in, w_ref[:, D_SSM + c * SSD_CONV_CHUNK:D_SSM + (c + 1) * SSD_CONV_CHUNK],
                    preferred_element_type=F32)
        zcols = slice(c * zw, (c + 1) * zw)
        z = jnp.dot(u_in, w_ref[:, zcols], preferred_element_type=F32)
        y = _causal_conv(x, cx_ref[:, cols], cw_ref[:, cols], cb_ref[:, cols])
        cx_ref[:, cols] = x[tm - SUBLANES:]
        xc_ref[:, cols] = jnp.where(valid, jax.nn.silu(y), 0.0)
        zs_ref[:, zcols] = jax.nn.silu(z)
    dt_raw = jnp.dot(u_in, w_ref[:, D_SSM + SSD_CONV_DIM:], preferred_element_type=F32)
    dt_ref[...] = jnp.where(valid, _softplus(dt_raw + dtb_ref[...]), 0.0)


def _l1_inproj(h, norm_w, w, conv_w, conv_b, dt_bias, tm):
    rowblk = lambda width: pl.BlockSpec((tm, width), lambda i: (i, 0))
    return pl.pallas_call(
        functools.partial(_l1_inproj_kernel, tm=tm, tpb=LP // tm),
        grid=(ROWS // tm,),
        in_specs=[rowblk(D_MODEL),
                  _const_spec((1, D_MODEL)),
                  _const_spec((D_MODEL, L1_IN_COLS)),
                  _const_spec((CONV_C, SSD_CONV_DIM)),
                  _const_spec((1, SSD_CONV_DIM)),
                  _const_spec((1, LANES))],
        out_specs=[rowblk(D_SSM), rowblk(SSD_CONV_DIM), rowblk(LANES)],
        out_shape=[jax.ShapeDtypeStruct((ROWS, D_SSM), F32),
                   jax.ShapeDtypeStruct((ROWS, SSD_CONV_DIM), F32),
                   jax.ShapeDtypeStruct((ROWS, LANES), F32)],
        scratch_shapes=[pltpu.VMEM((SUBLANES, SSD_CONV_DIM), F32)],
        compiler_params=_params(),
        name="l1_inproj",
    )(h, norm_w.reshape(1, D_MODEL), w, conv_w, conv_b.reshape(1, SSD_CONV_DIM),
      jnp.pad(dt_bias, (0, LANES - SSD_HEADS)).reshape(1, LANES))


def _outproj_kernel(*refs, widths, tm, tpb):
    a_refs = refs[:len(widths)]
    w_ref, h_ref, pw_ref, o_ref = refs[len(widths):]
    y = None
    r0 = 0
    for a_ref, width in zip(a_refs, widths):
        part = jnp.dot(a_ref[...], w_ref[r0:r0 + width, :], preferred_element_type=F32)
        y = part if y is None else y + part
        r0 += width
    o = h_ref[...] + _rms(y, pw_ref[...])
    o_ref[...] = jnp.where(_row_valid(pl.program_id(0) % tpb, tm), o, 0.0)


def _outproj(acts, w, h, post_w, tm, name):
    widths = tuple(a.shape[1] for a in acts)
    return pl.pallas_call(
        functools.partial(_outproj_kernel, widths=widths, tm=tm, tpb=LP // tm),
        grid=(ROWS // tm,),
        in_specs=[pl.BlockSpec((tm, k), lambda i: (i, 0)) for k in widths]
                 + [_const_spec((sum(widths), D_MODEL)),
                    pl.BlockSpec((tm, D_MODEL), lambda i: (i, 0)),
                    _const_spec((1, D_MODEL))],
        out_specs=pl.BlockSpec((tm, D_MODEL), lambda i: (i, 0)),
        out_shape=jax.ShapeDtypeStruct((ROWS, D_MODEL), F32),
        compiler_params=_params(),
        name=name,
    )(*acts, w, h, post_w.reshape(1, D_MODEL))


def _ffn_kernel(h_ref, nw_ref, wup_ref, cw_ref, cb_ref, wdn_ref, pw_ref, o_ref, carry_ref, act_ref, *, tm, tpb):
    tile = pl.program_id(0) % tpb

    @pl.when(tile == 0)
    def _():
        carry_ref[...] = jnp.zeros_like(carry_ref)

    h = h_ref[...]
    u = _rms(h, nw_ref[...]).astype(BF16)

    def conv_chunk(c0):
        cols = slice(c0, c0 + FF_CHUNK)
        x = jnp.dot(u, wup_ref[:, cols], preferred_element_type=F32)
        carry = carry_ref[:, cols]
        x1 = _with_history(carry, x, 1)
        x2 = _with_history(carry, x, 2)
        carry_ref[:, cols] = x[tm - SUBLANES:]
        w = cw_ref[:, cols]
        return cb_ref[:, cols] + w[2:3] * x + w[1:2] * x1 + w[0:1] * x2

    for j in range(N_FF_CHUNKS):
        g = conv_chunk(j * FF_CHUNK)
        v = conv_chunk(D_FF + j * FF_CHUNK)
        act_ref[:, j * FF_CHUNK:(j + 1) * FF_CHUNK] = (jax.nn.gelu(g, approximate=True) * v).astype(BF16)

    y = jnp.dot(act_ref[...], wdn_ref[...], preferred_element_type=F32)
    o = h + _rms(y, pw_ref[...])
    o_ref[...] = jnp.where(_row_valid(tile, tm), o, 0.0)


def _ffn(h, pre_w, w_up, conv_w, conv_b, w_down, post_w, tm, name):
    return pl.pallas_call(
        functools.partial(_ffn_kernel, tm=tm, tpb=LP // tm),
        grid=(ROWS // tm,),
        in_specs=[pl.BlockSpec((tm, D_MODEL), lambda i: (i, 0)),
                  _const_spec((1, D_MODEL)),
                  _const_spec((D_MODEL, 2 * D_FF)),
                  _const_spec((CONV_F, 2 * D_FF)),
                  _const_spec((1, 2 * D_FF)),
                  _const_spec((D_FF, D_MODEL)),
                  _const_spec((1, D_MODEL))],
        out_specs=pl.BlockSpec((tm, D_MODEL), lambda i: (i, 0)),
        out_shape=jax.ShapeDtypeStruct((ROWS, D_MODEL), F32),
        scratch_shapes=[pltpu.VMEM((SUBLANES, 2 * D_FF), F32),
                        pltpu.VMEM((tm, D_FF), BF16)],
        compiler_params=_params(),
        name=name,
    )(h, pre_w.reshape(1, D_MODEL), w_up.astype(BF16), conv_w, conv_b.reshape(1, 2 * D_FF),
      w_down.astype(BF16), post_w.reshape(1, D_MODEL))


def _linear_scan(a, u):
    n = a.shape[0]
    d = 1
    while d < n:
        a_s = _shift_rows(a, d, 1.0)
        u_s = _shift_rows(u, d, 0.0)
        u = a * u_s + u
        a = a * a_s
        d *= 2
    return a, u


def _lru_kernel(gg_ref, a_ref, u_ref, o_ref, ch_ref, *, tr):
    tile = pl.program_id(0) % (LP // tr)

    @pl.when(tile == 0)
    def _():
        ch_ref[...] = jnp.zeros_like(ch_ref)

    for n in range(LRU_BLOCKS):
        cols = slice(n * LRU_BS, (n + 1) * LRU_BS)
        a_cum, hloc = _linear_scan(a_ref[:, cols], u_ref[:, cols])
        h = hloc + a_cum * ch_ref[SUBLANES - 1:SUBLANES, cols]
        ch_ref[:, cols] = h[tr - SUBLANES:]
        o_ref[:, cols] = (gg_ref[:, cols] * h).astype(o_ref.dtype)


def _lru(gg, a, u, tr):
    rowblk = pl.BlockSpec((tr, D_RNN), lambda i: (i, 0))
    return pl.pallas_call(
        functools.partial(_lru_kernel, tr=tr),
        grid=(ROWS // tr,),
        in_specs=[rowblk, rowblk, rowblk],
        out_specs=rowblk,
        out_shape=jax.ShapeDtypeStruct((ROWS, D_RNN), BF16),
        scratch_shapes=[pltpu.VMEM((SUBLANES, D_RNN), F32)],
        compiler_params=_params(),
        name="l0_lru",
    )(gg, a, u)


def _attn_kernel(sink_ref, q_ref, km_ref, kp_ref, kc_ref, vm_ref, vp_ref, vc_ref, o_ref):
    n = pl.program_id(0) % NBLK
    jrow = lax.broadcasted_iota(jnp.int32, (BLOCK, BLOCK), 0)
    icol = lax.broadcasted_iota(jnp.int32, (BLOCK, BLOCK), 1)
    from_prev = jrow > icol
    neg_band = jnp.where(from_prev,
                         jnp.where(n >= 2, -(BLOCK + icol - jrow).astype(F32), -jnp.inf),
                         jnp.where(n >= 1, -(icol - jrow).astype(F32), -jnp.inf))
    m_row = lax.broadcasted_iota(jnp.int32, (N_META, BLOCK), 0)
    i_meta = lax.broadcasted_iota(jnp.int32, (N_META, BLOCK), 1)
    d_meta = n * BLOCK + i_meta - PAD - m_row
    neg_meta = jnp.where(d_meta >= 0, -jnp.minimum(d_meta, WINDOW).astype(F32), -jnp.inf)

    lane = lax.broadcasted_iota(jnp.int32, (1, LANES), 1)
    left = lane < HEAD_DIM
    halves = lambda x: (jnp.where(left, x, 0.0).astype(BF16), jnp.where(left, 0.0, x).astype(BF16))
    pairs_per_group = Q_PER_KV // 2
    scores, v_stacks = [], []
    for g in range(N_KV_HEADS):
        gcols = slice(g * LANES, (g + 1) * LANES)
        k_all = jnp.concatenate([kp_ref[:, gcols], kc_ref[:, gcols], km_ref[PAD:, gcols]], axis=0)
        k_all = k_all * (HEAD_DIM ** -0.5)
        v_left, v_right = halves(jnp.concatenate([vp_ref[:, gcols], vc_ref[:, gcols], vm_ref[PAD:, gcols]], axis=0))
        v_stacks.append(jnp.concatenate([v_left, v_right], axis=0))
        for pair in range(pairs_per_group):
            pcols = slice((g * pairs_per_group + pair) * LANES, (g * pairs_per_group + pair + 1) * LANES)
            q_left, q_right = halves(q_ref[:, pcols])
            q_both = jnp.concatenate([q_left, q_right], axis=0)
            scores.append(lax.dot_general(k_all, q_both, (((1,), (1,)), ((), ())), preferred_element_type=F32))
    for g in range(N_KV_HEADS):
        v_both = v_stacks[g]
        for pair in range(pairs_per_group):
            pcols = slice((g * pairs_per_group + pair) * LANES, (g * pairs_per_group + pair + 1) * LANES)
            s_t = scores[g * pairs_per_group + pair]
            p_parts = []
            for half in range(2):
                head = g * Q_PER_KV + 2 * pair + half
                slope = 2.0 ** (-8.0 * (head + 1) / N_Q_HEADS)
                sink = sink_ref[head]
                cols = slice(half * BLOCK, (half + 1) * BLOCK)
                band = jnp.where(from_prev, s_t[:BLOCK, cols], s_t[BLOCK:2 * BLOCK, cols]) + slope * neg_band
                meta = s_t[2 * BLOCK:, cols] + slope * neg_meta
                mx = jnp.maximum(jnp.maximum(jnp.max(band, axis=0, keepdims=True),
                                             jnp.max(meta, axis=0, keepdims=True)), sink)
                e_band = jnp.exp(band - mx)
                e_meta = jnp.exp(meta - mx)
                denom = (jnp.sum(e_band, axis=0, keepdims=True) + jnp.sum(e_meta, axis=0, keepdims=True)
                         + jnp.exp(sink - mx))
                r = 1.0 / denom
                p_band = e_band * r
                p_parts += [jnp.where(from_prev, p_band, 0.0).astype(BF16),
                            jnp.where(from_prev, 0.0, p_band).astype(BF16),
                            (e_meta * r).astype(BF16)]
            p_t = jnp.concatenate(p_parts, axis=0)
            o_pair = lax.dot_general(p_t, v_both, (((0,), (0,)), ((), ())), preferred_element_type=F32)
            o_ref[:, pcols] = o_pair.astype(o_ref.dtype)


def _attention(qkv, sinks):
    kcol = Q_DIM // (2 * LANES)
    vcol = kcol + 1
    first = lambda i: (i // NBLK) * NBLK
    prev = lambda i: jnp.maximum(i - 1, 0)
    kv = lambda rowf, col: pl.BlockSpec((BLOCK, 2 * LANES), lambda i: (rowf(i), col))
    return pl.pallas_call(
        _attn_kernel,
        grid=(ROWS // BLOCK,),
        in_specs=[pl.BlockSpec(memory_space=pltpu.SMEM),
                  pl.BlockSpec((BLOCK, Q_DIM), lambda i: (i, 0)),
                  kv(first, kcol), kv(prev, kcol), kv(lambda i: i, kcol),
                  kv(first, vcol), kv(prev, vcol), kv(lambda i: i, vcol)],
        out_specs=pl.BlockSpec((BLOCK, Q_DIM), lambda i: (i, 0)),
        out_shape=jax.ShapeDtypeStruct((ROWS, Q_DIM), BF16),
        compiler_params=_params(),
        name="l0_attn",
    )(sinks, qkv, qkv, qkv, qkv, qkv, qkv, qkv)


def _ssd_kernel(zs_ref, xc_ref, dt_ref, alog_ref, dskip_ref, gn_ref, o_ref, state_ref):
    n = pl.program_id(0) % NBLK

    @pl.when(n == 0)
    def _():
        state_ref[...] = jnp.zeros_like(state_ref)

    dt = dt_ref[...]
    a = -jnp.exp(alog_ref[...])
    cs = dt * a
    d = 1
    while d < BLOCK:
        cs = cs + _shift_rows(cs, d, 0.0)
        d *= 2
    cs_last = cs[BLOCK - 1:BLOCK, :]
    dtw = dt * jnp.exp(cs_last - cs)
    chunk_decay = jnp.exp(cs_last)
    cs_t = cs.T
    dt_t = dt.T
    dtw_t = dtw.T

    row = lax.broadcasted_iota(jnp.int32, (BLOCK, BLOCK), 0)
    col = lax.broadcasted_iota(jnp.int32, (BLOCK, BLOCK), 1)
    tril = row >= col
    lane = lax.broadcasted_iota(jnp.int32, (1, LANES), 1)
    left = lane < SSD_HEADDIM

    gw = SSD_HPG * SSD_HEADDIM
    b_ts, cbs, y_offs = [], [], []
    for g in range(SSD_GROUPS):
        b_g = xc_ref[:, D_SSM + g * SSD_STATE:D_SSM + (g + 1) * SSD_STATE]
        c_g = xc_ref[:, D_SSM + SSD_BC + g * SSD_STATE:D_SSM + SSD_BC + (g + 1) * SSD_STATE].astype(BF16)
        b_t = b_g.T
        b_ts.append(b_t)
        cbs.append(jnp.dot(c_g, b_t.astype(BF16), preferred_element_type=F32))
        y_offs.append(jnp.dot(c_g, state_ref[g].astype(BF16), preferred_element_type=F32))
    pairs = [(g, pair) for g in range(SSD_GROUPS) for pair in range(SSD_HPG // 2)]
    pair_cols = lambda g, pair: slice(g * gw + pair * LANES, g * gw + (pair + 1) * LANES)
    lhs_all, e_cs_all = [], []
    for g, pair in pairs:
        h0 = g * SSD_HPG + 2 * pair
        lhs_top, lhs_bot, e_cs = [], [], []
        for h in (h0, h0 + 1):
            cs_col = jnp.broadcast_to(cs[:, h:h + 1], (BLOCK, BLOCK))
            seg = cs_col - cs_t[h:h + 1, :]
            decay = jnp.exp(jnp.where(tril, seg, -jnp.inf))
            lhs_top.append((cbs[g] * decay * dt_t[h:h + 1, :]).astype(BF16))
            lhs_bot.append((b_ts[g] * dtw_t[h:h + 1, :]).astype(BF16))
            e_cs.append(jnp.exp(cs_col))
        lhs_all.append(jnp.concatenate([jnp.concatenate(lhs_top, axis=1), jnp.concatenate(lhs_bot, axis=1)], axis=0))
        e_cs_all.append(jnp.where(left, e_cs[0], e_cs[1]))
    res_all = []
    for (g, pair), lhs in zip(pairs, lhs_all):
        xp = xc_ref[:, pair_cols(g, pair)]
        x_bd = jnp.concatenate([jnp.where(left, xp, 0.0), jnp.where(left, 0.0, xp)], axis=0).astype(BF16)
        res_all.append(jnp.dot(lhs, x_bd, preferred_element_type=F32))
    for g in range(SSD_GROUPS):
        y_parts = []
        for pair in range(SSD_HPG // 2):
            h0 = g * SSD_HPG + 2 * pair
            idx = g * (SSD_HPG // 2) + pair
            res = res_all[idx]
            pcols = pair_cols(g, pair)
            scols = slice(pair * LANES, (pair + 1) * LANES)
            y_pair = res[:BLOCK] + y_offs[g][:, scols] * e_cs_all[idx]
            dec = jnp.where(left, chunk_decay[:, h0:h0 + 1], chunk_decay[:, h0 + 1:h0 + 2])
            state_ref[g, :, scols] = state_ref[g, :, scols] * dec + res[BLOCK:]
            y_parts.append(y_pair + dskip_ref[:, pcols] * xc_ref[:, pcols])
        gcols = slice(g * gw, (g + 1) * gw)
        y_g = jnp.concatenate(y_parts, axis=1) * zs_ref[:, gcols]
        y_g = y_g * lax.rsqrt(jnp.mean(y_g * y_g, axis=-1, keepdims=True) + EPS)
        o_ref[:, gcols] = (y_g * gn_ref[:, gcols]).astype(o_ref.dtype)


def _ssd(zs, xc, dt, a_log, d_skip, gate_norm):
    rowblk = lambda w: pl.BlockSpec((BLOCK, w), lambda i: (i, 0))
    gw = SSD_HPG * SSD_HEADDIM
    return pl.pallas_call(
        _ssd_kernel,
        grid=(ROWS // BLOCK,),
        in_specs=[rowblk(D_SSM), rowblk(SSD_CONV_DIM), rowblk(LANES),
                  _const_spec((1, LANES)),
                  _const_spec((1, D_SSM)),
                  _const_spec((1, D_SSM))],
        out_specs=rowblk(D_SSM),
        out_shape=jax.ShapeDtypeStruct((ROWS, D_SSM), BF16),
        scratch_shapes=[pltpu.VMEM((SSD_GROUPS, SSD_STATE, gw), F32)],
        compiler_params=_params(),
        name="l1_ssd",
    )(zs, xc, dt, jnp.pad(a_log, (0, LANES - SSD_HEADS)).reshape(1, LANES),
      jnp.repeat(d_skip, SSD_HEADDIM).reshape(1, D_SSM), gate_norm.reshape(1, D_SSM))


def kernel(x, meta_tokens,
           l0_mix_pre_norm, l0_mix_post_norm, l0_w_in, l0_lru_conv_w, l0_lru_conv_b,
           l0_lru_w_a, l0_lru_b_a, l0_lru_w_x, l0_lru_b_x, l0_lru_lambda, l0_attn_sinks, l0_w_out,
           l0_ffn_pre_norm, l0_ffn_post_norm, l0_ffn_w_up, l0_ffn_conv_w, l0_ffn_conv_b, l0_ffn_w_down,
           l1_mix_pre_norm, l1_mix_post_norm, l1_w_in, l1_ssm_conv_w, l1_ssm_conv_b,
           l1_dt_bias, l1_a_log, l1_d_skip, l1_gate_norm, l1_w_out,
           l1_ffn_pre_norm, l1_ffn_post_norm, l1_ffn_w_up, l1_ffn_conv_w, l1_ffn_conv_b, l1_ffn_w_down):
    bsz = x.shape[0]
    meta = jnp.broadcast_to(meta_tokens.astype(x.dtype)[None], (bsz, N_META, D_MODEL))
    h = jnp.concatenate([jnp.zeros((bsz, PAD, D_MODEL), x.dtype), meta, x], axis=1).reshape(ROWS, D_MODEL)

    kq = 2 * D_RNN + Q_DIM
    dup = lambda w: jnp.concatenate([w[:, :HEAD_DIM]] * 2 + [w[:, HEAD_DIM:]] * 2, axis=1)
    w0 = jnp.concatenate([l0_w_in[:, :kq], dup(l0_w_in[:, kq:kq + KV_DIM]), dup(l0_w_in[:, kq + KV_DIM:])],
                         axis=1).astype(BF16)
    gg, a, u, qkv = _l0_inproj(h, l0_mix_pre_norm, w0, l0_lru_conv_w, l0_lru_conv_b, l0_lru_w_a, l0_lru_b_a,
                               l0_lru_w_x, l0_lru_b_x, l0_lru_lambda, 640)
    y_a = _lru(gg, a, u, BLOCK)
    y_b = _attention(qkv, l0_attn_sinks)
    h = _outproj([y_a, y_b], l0_w_out.astype(BF16), h, l0_mix_post_norm, 640, "l0_outproj")
    h = _ffn(h, l0_ffn_pre_norm, l0_ffn_w_up, l0_ffn_conv_w, l0_ffn_conv_b, l0_ffn_w_down, l0_ffn_post_norm,
             640, "l0_ffn")

    w1 = jnp.pad(l1_w_in, ((0, 0), (0, LANES - SSD_HEADS))).astype(BF16)
    zs, xc, dt = _l1_inproj(h, l1_mix_pre_norm, w1, l1_ssm_conv_w, l1_ssm_conv_b, l1_dt_bias, 320)
    y = _ssd(zs, xc, dt, l1_a_log, l1_d_skip, l1_gate_norm)
    h = _outproj([y], l1_w_out.astype(BF16), h, l1_mix_post_norm, 640, "l1_outproj")
    h = _ffn(h, l1_ffn_pre_norm, l1_ffn_w_up, l1_ffn_conv_w, l1_ffn_conv_b, l1_ffn_w_down, l1_ffn_post_norm,
             640, "l1_ffn")
    return h.reshape(bsz, LP, D_MODEL)[:, BLOCK:]
```

```python
import functools
import math

import jax
import jax.numpy as jnp
from jax import lax
from jax.experimental import pallas as pl
from jax.experimental.pallas import tpu as pltpu

D_MODEL = 1024
BATCH = 2
SEQ = 8192
N_META = 16
BLOCK = 128
PAD = BLOCK - N_META
LP = PAD + N_META + SEQ
NBLK = LP // BLOCK
ROWS = BATCH * LP

D_RNN = D_MODEL
LRU_BLOCKS = 8
LRU_BS = D_RNN // LRU_BLOCKS
LRU_C = 8.0
CONV_A = 4

N_Q_HEADS = 16
N_KV_HEADS = 2
HEAD_DIM = 64
Q_PER_KV = N_Q_HEADS // N_KV_HEADS
WINDOW = 128
Q_DIM = N_Q_HEADS * HEAD_DIM
KV_DIM = N_KV_HEADS * HEAD_DIM

D_SSM = 2 * D_MODEL
SSD_HEADDIM = 64
SSD_HEADS = D_SSM // SSD_HEADDIM
SSD_GROUPS = 8
SSD_HPG = SSD_HEADS // SSD_GROUPS
SSD_STATE = 128
CONV_C = 4
SSD_BC = SSD_GROUPS * SSD_STATE
SSD_CONV_DIM = D_SSM + 2 * SSD_BC

D_FF = 2816
CONV_F = 3
FF_CHUNK = 256
N_FF_CHUNKS = D_FF // FF_CHUNK

EPS = 1e-6
LANES = 128
SUBLANES = 8

VMEM_LIMIT = 56 * 1024 * 1024

F32 = jnp.float32
BF16 = jnp.bfloat16


def _rms(x, w):
    return x * lax.rsqrt(jnp.mean(x * x, axis=-1, keepdims=True) + EPS) * w


def _softplus(y):
    return jnp.maximum(y, 0.0) + jnp.log1p(jnp.exp(-jnp.abs(y)))


def _shift_rows(x, d, fill):
    n, c = x.shape
    if d % SUBLANES == 0:
        return jnp.concatenate([jnp.full((d, c), fill, x.dtype), x[: n - d]], axis=0)
    row = lax.broadcasted_iota(jnp.int32, x.shape, 0)
    return jnp.where(row >= d, pltpu.roll(x, d, 0), fill)


def _with_history(carry, x, k):
    xx = jnp.concatenate([carry, x], axis=0)
    return pltpu.roll(xx, k, 0)[SUBLANES:]


def _row_valid(tile_in_batch, tm):
    row = tile_in_batch * tm + lax.broadcasted_iota(jnp.int32, (tm, 1), 0)
    return row >= PAD


def _const_spec(shape):
    nd = len(shape)
    return pl.BlockSpec(shape, lambda i: (0,) * nd, pipeline_mode=pl.Buffered(1))


def _params():
    return pltpu.CompilerParams(dimension_semantics=("arbitrary",), vmem_limit_bytes=VMEM_LIMIT)


def _causal_conv(x, carry, cw, cb):
    taps = cw.shape[0]
    y = cb + cw[taps - 1:taps] * x
    for k in range(1, taps):
        y = y + cw[taps - 1 - k:taps - k] * _with_history(carry, x, k)
    return y


L0_IN_COLS = 2 * D_RNN + Q_DIM + 4 * KV_DIM
LRU_CHUNK = 2 * LRU_BS


def _token_block_spec(k, tm, pad_block_target):
    bpt, tpb, xb = tm // BLOCK, LP // tm, SEQ // BLOCK

    def index_map(i):
        n = (i % tpb) * bpt + k
        return ((i // tpb) * xb + jnp.where(n == 0, pad_block_target, n - 1), 0)

    return pl.BlockSpec((BLOCK, D_MODEL), index_map)


def _initial_stream_tile(tile, x_refs, meta_ref):
    first = jnp.concatenate([jnp.zeros((PAD, D_MODEL), F32), meta_ref[...]], axis=0)
    return jnp.concatenate([jnp.where(tile == 0, first, x_refs[0][...])] + [r[...] for r in x_refs[1:]], axis=0)


def _l0_inproj_kernel(*refs, tm, tpb):
    nx = tm // BLOCK
    x_refs = refs[:nx]
    (meta_ref, nw_ref, w_ref, cw_ref, cb_ref, wg_ref, ba_ref, bx_ref, lam_ref,
     gg_ref, a_ref, u_ref, qkv_ref, cx_ref) = refs[nx:]
    tile = pl.program_id(0) % tpb

    @pl.when(tile == 0)
    def _():
        cx_ref[...] = jnp.zeros_like(cx_ref)

    u_in = _rms(_initial_stream_tile(tile, x_refs, meta_ref), nw_ref[...]).astype(BF16)
    valid = _row_valid(tile, tm)
    xs = [jnp.dot(u_in, w_ref[:, D_RNN + c * LRU_CHUNK:D_RNN + (c + 1) * LRU_CHUNK], preferred_element_type=F32)
          for c in range(D_RNN // LRU_CHUNK)]
    gate = jnp.dot(u_in, w_ref[:, :D_RNN], preferred_element_type=F32)
    xcs = []
    for c, x in enumerate(xs):
        cols = slice(c * LRU_CHUNK, (c + 1) * LRU_CHUNK)
        xcs.append(_causal_conv(x, cx_ref[:, cols], cw_ref[:, cols], cb_ref[:, cols]))
        cx_ref[:, cols] = x[tm - SUBLANES:]
    pres = [jnp.dot(xcs[n // 2][:, (n % 2) * LRU_BS:(n % 2 + 1) * LRU_BS].astype(BF16), wg_ref[n],
                    preferred_element_type=F32) for n in range(LRU_BLOCKS)]
    qkv_ref[...] = jnp.dot(u_in, w_ref[:, 2 * D_RNN:], preferred_element_type=F32).astype(qkv_ref.dtype)
    for n in range(LRU_BLOCKS):
        cols = slice(n * LRU_BS, (n + 1) * LRU_BS)
        xc = xcs[n // 2][:, (n % 2) * LRU_BS:(n % 2 + 1) * LRU_BS]
        r = jax.nn.sigmoid(pres[n][:, :LRU_BS] + ba_ref[:, cols])
        i = jax.nn.sigmoid(pres[n][:, LRU_BS:] + bx_ref[:, cols])
        log_a = (-LRU_C * _softplus(-lam_ref[:, cols])) * r
        a = jnp.exp(log_a)
        a_ref[:, cols] = a
        u_ref[:, cols] = jnp.where(valid, jnp.sqrt(1.0 - a * a) * (i * xc), 0.0)
    gg_ref[...] = jax.nn.gelu(gate, approximate=True)


def _l0_inproj(x2d, meta, norm_w, w, conv_w, conv_b, w_a, b_a, w_x, b_x, lam, tm):
    wg = jnp.concatenate([w_a, w_x], axis=-1).astype(BF16)
    row = lambda v: v.reshape(1, D_RNN)
    rowblk = lambda width: pl.BlockSpec((tm, width), lambda i: (i, 0))
    nx = tm // BLOCK
    return pl.pallas_call(
        functools.partial(_l0_inproj_kernel, tm=tm, tpb=LP // tm),
        grid=(ROWS // tm,),
        in_specs=[_token_block_spec(k, tm, 0) for k in range(nx)]
                 + [_const_spec((N_META, D_MODEL)),
                  _const_spec((1, D_MODEL)),
                  _const_spec((D_MODEL, L0_IN_COLS)),
                  _const_spec((CONV_A, D_RNN)),
                  _const_spec((1, D_RNN)),
                  _const_spec((LRU_BLOCKS, LRU_BS, 2 * LRU_BS)),
                  _const_spec((1, D_RNN)),
                  _const_spec((1, D_RNN)),
                  _const_spec((1, D_RNN))],
        out_specs=[rowblk(D_RNN), rowblk(D_RNN), rowblk(D_RNN), rowblk(Q_DIM + 4 * KV_DIM)],
        out_shape=[jax.ShapeDtypeStruct((ROWS, D_RNN), F32)] * 3
                  + [jax.ShapeDtypeStruct((ROWS, Q_DIM + 4 * KV_DIM), BF16)],
        scratch_shapes=[pltpu.VMEM((SUBLANES, D_RNN), F32)],
        compiler_params=_params(),
        name="l0_inproj",
    )(*([x2d] * nx), meta, norm_w.reshape(1, D_MODEL), w, conv_w, row(conv_b), wg, row(b_a), row(b_x), row(lam))


L1_IN_COLS = D_SSM + SSD_CONV_DIM + LANES
SSD_CONV_CHUNK = 4 * LANES


def _l1_inproj_kernel(h_ref, nw_ref, w_ref, cw_ref, cb_ref, dtb_ref, zs_ref, xc_ref, dt_ref, *xs_refs, tm, tpb):
    tile = pl.program_id(0) % tpb
    q = tm // SUBLANES
    spc = SSD_CONV_CHUNK // LANES
    n_chunks = SSD_CONV_DIM // SSD_CONV_CHUNK

    @pl.when(tile == 0)
    def _():
        for xs_ref in xs_refs:
            xs_ref[:, 0:SUBLANES, :] = jnp.zeros((spc, SUBLANES, LANES), F32)

    u_in = _rms(h_ref[...], nw_ref[...]).astype(BF16)
    sub = lax.broadcasted_iota(jnp.int32, (SUBLANES, LANES), 0)
    row0 = tile * tm + sub * q
    valid_p = [row0 + v >= PAD for v in range(q)]
    zw = D_SSM // n_chunks
    for c in range(n_chunks):
        xs_ref = xs_refs[c]
        x = jnp.dot(u_in, w_ref[:, D_SSM + c * SSD_CONV_CHUNK:D_SSM + (c + 1) * SSD_CONV_CHUNK],
                    preferred_element_type=F32)
        for j in range(spc):
            xs_ref[j, SUBLANES:SUBLANES + tm, :] = x[:, j * LANES:(j + 1) * LANES]
        zcols = slice(c * zw, (c + 1) * zw)
        zs_ref[:, zcols] = jax.nn.silu(jnp.dot(u_in, w_ref[:, zcols], preferred_element_type=F32))
        for j in range(spc):
            lcols = slice(c * SSD_CONV_CHUNK + j * LANES, c * SSD_CONV_CHUNK + (j + 1) * LANES)
            wk = [jnp.broadcast_to(cw_ref[k:k + 1, lcols], (SUBLANES, LANES)) for k in range(CONV_C)]
            bias = jnp.broadcast_to(cb_ref[:, lcols], (SUBLANES, LANES))
            load = lambda start: xs_ref[j, pl.ds(start, SUBLANES, stride=q), :]
            window = [load(SUBLANES - k) for k in range(CONV_C - 1, 0, -1)]
            for v in range(q):
                window.append(load(SUBLANES + v))
                acc = bias
                for k in range(CONV_C):
                    acc = acc + wk[k] * window[k]
                window.pop(0)
                xc_ref[c * spc + j, pl.ds(v, SUBLANES, stride=q), :] = jnp.where(valid_p[v], jax.nn.silu(acc), 0.0)
        for j in range(spc):
            xs_ref[j, 0:SUBLANES, :] = xs_ref[j, tm:tm + SUBLANES, :]
    dt_raw = jnp.dot(u_in, w_ref[:, D_SSM + SSD_CONV_DIM:], preferred_element_type=F32)
    dt_ref[...] = jnp.where(_row_valid(tile, tm), _softplus(dt_raw + dtb_ref[...]), 0.0)


def _l1_inproj(h, norm_w, w, conv_w, conv_b, dt_bias, tm):
    rowblk = lambda width: pl.BlockSpec((tm, width), lambda i: (i, 0))
    n_slabs = SSD_CONV_DIM // LANES
    return pl.pallas_call(
        functools.partial(_l1_inproj_kernel, tm=tm, tpb=LP // tm),
        grid=(ROWS // tm,),
        in_specs=[rowblk(D_MODEL),
                  _const_spec((1, D_MODEL)),
                  _const_spec((D_MODEL, L1_IN_COLS)),
                  _const_spec((CONV_C, SSD_CONV_DIM)),
                  _const_spec((1, SSD_CONV_DIM)),
                  _const_spec((1, LANES))],
        out_specs=[rowblk(D_SSM), pl.BlockSpec((n_slabs, tm, LANES), lambda i: (0, i, 0)), rowblk(LANES)],
        out_shape=[jax.ShapeDtypeStruct((ROWS, D_SSM), F32),
                   jax.ShapeDtypeStruct((n_slabs, ROWS, LANES), F32),
                   jax.ShapeDtypeStruct((ROWS, LANES), F32)],
        scratch_shapes=[pltpu.VMEM((SSD_CONV_CHUNK // LANES, SUBLANES + tm, LANES), F32)
                        for _ in range(SSD_CONV_DIM // SSD_CONV_CHUNK)],
        compiler_params=_params(),
        name="l1_inproj",
    )(h, norm_w.reshape(1, D_MODEL), w, conv_w, conv_b.reshape(1, SSD_CONV_DIM),
      jnp.pad(dt_bias, (0, LANES - SSD_HEADS)).reshape(1, LANES))


def _outproj_kernel(*refs, widths, from_tokens, tm, tpb):
    a_refs = refs[:len(widths)]
    w_ref, pw_ref = refs[len(widths):len(widths) + 2]
    res_refs = refs[len(widths) + 2:-1]
    o_ref = refs[-1]
    tile = pl.program_id(0) % tpb
    y = None
    r0 = 0
    for a_ref, width in zip(a_refs, widths):
        part = jnp.dot(a_ref[...], w_ref[r0:r0 + width, :], preferred_element_type=F32)
        y = part if y is None else y + part
        r0 += width
    h = _initial_stream_tile(tile, res_refs[:-1], res_refs[-1]) if from_tokens else res_refs[0][...]
    o_ref[...] = jnp.where(_row_valid(tile, tm), h + _rms(y, pw_ref[...]), 0.0)


def _outproj(acts, w, residual, post_w, tm, name):
    widths = tuple(a.shape[1] for a in acts)
    from_tokens = isinstance(residual, tuple)
    if from_tokens:
        nx = tm // BLOCK
        res_specs = [_token_block_spec(k, tm, 0) for k in range(nx)] + [_const_spec((N_META, D_MODEL))]
        res_args = [residual[0]] * nx + [residual[1]]
    else:
        res_specs = [pl.BlockSpec((tm, D_MODEL), lambda i: (i, 0))]
        res_args = [residual]
    return pl.pallas_call(
        functools.partial(_outproj_kernel, widths=widths, from_tokens=from_tokens, tm=tm, tpb=LP // tm),
        grid=(ROWS // tm,),
        in_specs=[pl.BlockSpec((tm, k), lambda i: (i, 0)) for k in widths]
                 + [_const_spec((sum(widths), D_MODEL)), _const_spec((1, D_MODEL))] + res_specs,
        out_specs=pl.BlockSpec((tm, D_MODEL), lambda i: (i, 0)),
        out_shape=jax.ShapeDtypeStruct((ROWS, D_MODEL), F32),
        compiler_params=_params(),
        name=name,
    )(*acts, w, post_w.reshape(1, D_MODEL), *res_args)


def _ffn_kernel(h_ref, nw_ref, wup_ref, cw_ref, cb_ref, wdn_ref, pw_ref, o_ref, carry_ref, act_ref, *, tm, tpb):
    tile = pl.program_id(0) % tpb

    @pl.when(tile == 0)
    def _():
        carry_ref[...] = jnp.zeros_like(carry_ref)

    h = h_ref[...]
    u = _rms(h, nw_ref[...]).astype(BF16)

    def conv_chunk(c0):
        cols = slice(c0, c0 + FF_CHUNK)
        x = jnp.dot(u, wup_ref[:, cols], preferred_element_type=F32)
        carry = carry_ref[:, cols]
        x1 = _with_history(carry, x, 1)
        x2 = _with_history(carry, x, 2)
        carry_ref[:, cols] = x[tm - SUBLANES:]
        w = cw_ref[:, cols]
        return cb_ref[:, cols] + w[2:3] * x + w[1:2] * x1 + w[0:1] * x2

    for j in range(N_FF_CHUNKS):
        g = conv_chunk(j * FF_CHUNK)
        v = conv_chunk(D_FF + j * FF_CHUNK)
        act_ref[:, j * FF_CHUNK:(j + 1) * FF_CHUNK] = (jax.nn.gelu(g, approximate=True) * v).astype(BF16)

    y = jnp.dot(act_ref[...], wdn_ref[...], preferred_element_type=F32)
    o_ref[...] = jnp.where(_row_valid(tile, tm), h + _rms(y, pw_ref[...]), 0.0)


def _ffn(h, pre_w, w_up, conv_w, conv_b, w_down, post_w, tm, name):
    return pl.pallas_call(
        functools.partial(_ffn_kernel, tm=tm, tpb=LP // tm),
        grid=(ROWS // tm,),
        in_specs=[pl.BlockSpec((tm, D_MODEL), lambda i: (i, 0)),
                  _const_spec((1, D_MODEL)),
                  _const_spec((D_MODEL, 2 * D_FF)),
                  _const_spec((CONV_F, 2 * D_FF)),
                  _const_spec((1, 2 * D_FF)),
                  _const_spec((D_FF, D_MODEL)),
                  _const_spec((1, D_MODEL))],
        out_specs=pl.BlockSpec((tm, D_MODEL), lambda i: (i, 0)),
        out_shape=jax.ShapeDtypeStruct((ROWS, D_MODEL), F32),
        scratch_shapes=[pltpu.VMEM((SUBLANES, 2 * D_FF), F32),
                        pltpu.VMEM((tm, D_FF), BF16)],
        compiler_params=_params(),
        name=name,
    )(h, pre_w.reshape(1, D_MODEL), w_up.astype(BF16), conv_w, conv_b.reshape(1, 2 * D_FF),
      w_down.astype(BF16), post_w.reshape(1, D_MODEL))


def _linear_scan(a, u):
    n = a.shape[0]
    d = 1
    while d < n:
        a_s = _shift_rows(a, d, 1.0)
        u_s = _shift_rows(u, d, 0.0)
        u = a * u_s + u
        a = a * a_s
        d *= 2
    return a, u


def _lru_kernel(gg_ref, a_ref, u_ref, o_ref, ch_ref, *, tr):
    tile = pl.program_id(0) % (LP // tr)

    @pl.when(tile == 0)
    def _():
        ch_ref[...] = jnp.zeros_like(ch_ref)

    for n in range(LRU_BLOCKS):
        cols = slice(n * LRU_BS, (n + 1) * LRU_BS)
        carry = ch_ref[SUBLANES - 1:SUBLANES, cols]
        for r0 in range(0, tr, BLOCK):
            rows = slice(r0, r0 + BLOCK)
            a_cum, hloc = _linear_scan(a_ref[rows, cols], u_ref[rows, cols])
            h = hloc + a_cum * carry
            carry = h[BLOCK - 1:BLOCK]
            o_ref[rows, cols] = (gg_ref[rows, cols] * h).astype(o_ref.dtype)
        ch_ref[:, cols] = jnp.broadcast_to(carry, (SUBLANES, LRU_BS))


def _lru(gg, a, u, tr):
    rowblk = pl.BlockSpec((tr, D_RNN), lambda i: (i, 0))
    return pl.pallas_call(
        functools.partial(_lru_kernel, tr=tr),
        grid=(ROWS // tr,),
        in_specs=[rowblk, rowblk, rowblk],
        out_specs=rowblk,
        out_shape=jax.ShapeDtypeStruct((ROWS, D_RNN), BF16),
        scratch_shapes=[pltpu.VMEM((SUBLANES, D_RNN), F32)],
        compiler_params=_params(),
        name="l0_lru",
    )(gg, a, u)


def _attn_kernel(sink_ref, q_ref, km_ref, kp_ref, kc_ref, vm_ref, vp_ref, vc_ref, o_ref):
    n = pl.program_id(0) % NBLK
    jrow = lax.broadcasted_iota(jnp.int32, (BLOCK, BLOCK), 0)
    icol = lax.broadcasted_iota(jnp.int32, (BLOCK, BLOCK), 1)
    from_prev = jrow > icol
    neg_band = jnp.where(from_prev,
                         jnp.where(n >= 2, -(BLOCK + icol - jrow).astype(F32), -jnp.inf),
                         jnp.where(n >= 1, -(icol - jrow).astype(F32), -jnp.inf))
    m_row = lax.broadcasted_iota(jnp.int32, (N_META, BLOCK), 0)
    i_meta = lax.broadcasted_iota(jnp.int32, (N_META, BLOCK), 1)
    d_meta = n * BLOCK + i_meta - PAD - m_row
    neg_meta = jnp.where(d_meta >= 0, -jnp.minimum(d_meta, WINDOW).astype(F32), -jnp.inf)

    lane = lax.broadcasted_iota(jnp.int32, (1, LANES), 1)
    left = lane < HEAD_DIM
    halves = lambda x: (jnp.where(left, x, 0.0).astype(BF16), jnp.where(left, 0.0, x).astype(BF16))
    pairs_per_group = Q_PER_KV // 2
    scores, v_stacks = [], []
    for g in range(N_KV_HEADS):
        gcols = slice(g * LANES, (g + 1) * LANES)
        k_all = jnp.concatenate([kp_ref[:, gcols], kc_ref[:, gcols], km_ref[PAD:, gcols]], axis=0)
        k_all = k_all * (HEAD_DIM ** -0.5)
        v_left, v_right = halves(jnp.concatenate([vp_ref[:, gcols], vc_ref[:, gcols], vm_ref[PAD:, gcols]], axis=0))
        v_stacks.append(jnp.concatenate([v_left, v_right], axis=0))
        for pair in range(pairs_per_group):
            pcols = slice((g * pairs_per_group + pair) * LANES, (g * pairs_per_group + pair + 1) * LANES)
            q_left, q_right = halves(q_ref[:, pcols])
            q_both = jnp.concatenate([q_left, q_right], axis=0)
            scores.append(lax.dot_general(k_all, q_both, (((1,), (1,)), ((), ())), preferred_element_type=F32))
    for g in range(N_KV_HEADS):
        v_both = v_stacks[g]
        for pair in range(pairs_per_group):
            pcols = slice((g * pairs_per_group + pair) * LANES, (g * pairs_per_group + pair + 1) * LANES)
            s_t = scores[g * pairs_per_group + pair]
            p_parts = []
            for half in range(2):
                head = g * Q_PER_KV + 2 * pair + half
                slope = 2.0 ** (-8.0 * (head + 1) / N_Q_HEADS)
                sink = sink_ref[head]
                cols = slice(half * BLOCK, (half + 1) * BLOCK)
                band = jnp.where(from_prev, s_t[:BLOCK, cols], s_t[BLOCK:2 * BLOCK, cols]) + slope * neg_band
                meta = s_t[2 * BLOCK:, cols] + slope * neg_meta
                mx = jnp.maximum(jnp.maximum(jnp.max(band, axis=0, keepdims=True),
                                             jnp.max(meta, axis=0, keepdims=True)), sink)
                e_band = jnp.exp(band - mx)
                e_meta = jnp.exp(meta - mx)
                denom = (jnp.sum(e_band, axis=0, keepdims=True) + jnp.sum(e_meta, axis=0, keepdims=True)
                         + jnp.exp(sink - mx))
                r = 1.0 / denom
                p_band = e_band * r
                p_parts += [jnp.where(from_prev, p_band, 0.0).astype(BF16),
                            jnp.where(from_prev, 0.0, p_band).astype(BF16),
                            (e_meta * r).astype(BF16)]
            p_t = jnp.concatenate(p_parts, axis=0)
            o_pair = lax.dot_general(p_t, v_both, (((0,), (0,)), ((), ())), preferred_element_type=F32)
            o_ref[:, pcols] = o_pair.astype(o_ref.dtype)


def _attention(qkv, sinks):
    kcol = Q_DIM // (2 * LANES)
    vcol = kcol + 1
    first = lambda i: (i // NBLK) * NBLK
    prev = lambda i: jnp.maximum(i - 1, 0)
    kv = lambda rowf, col: pl.BlockSpec((BLOCK, 2 * LANES), lambda i: (rowf(i), col))
    return pl.pallas_call(
        _attn_kernel,
        grid=(ROWS // BLOCK,),
        in_specs=[pl.BlockSpec(memory_space=pltpu.SMEM),
                  pl.BlockSpec((BLOCK, Q_DIM), lambda i: (i, 0)),
                  kv(first, kcol), kv(prev, kcol), kv(lambda i: i, kcol),
                  kv(first, vcol), kv(prev, vcol), kv(lambda i: i, vcol)],
        out_specs=pl.BlockSpec((BLOCK, Q_DIM), lambda i: (i, 0)),
        out_shape=jax.ShapeDtypeStruct((ROWS, Q_DIM), BF16),
        compiler_params=_params(),
        name="l0_attn",
    )(sinks, qkv, qkv, qkv, qkv, qkv, qkv, qkv)


def _ssd_kernel(zs_ref, xc_ref, dt_ref, alog_ref, dskip_ref, gn_ref, o_ref, state_ref):
    n = pl.program_id(0) % NBLK

    @pl.when(n == 0)
    def _():
        state_ref[...] = jnp.zeros_like(state_ref)

    dt = dt_ref[...]
    a = -jnp.exp(alog_ref[...])
    cs = dt * a
    d = 1
    while d < BLOCK:
        cs = cs + _shift_rows(cs, d, 0.0)
        d *= 2
    cs_last = cs[BLOCK - 1:BLOCK, :]
    dtw = dt * jnp.exp(cs_last - cs)
    chunk_decay = jnp.exp(cs_last)
    cs_t = cs.T
    dt_t = dt.T
    dtw_t = dtw.T

    row = lax.broadcasted_iota(jnp.int32, (BLOCK, BLOCK), 0)
    col = lax.broadcasted_iota(jnp.int32, (BLOCK, BLOCK), 1)
    tril = row >= col
    lane = lax.broadcasted_iota(jnp.int32, (1, LANES), 1)
    left = lane < SSD_HEADDIM

    gw = SSD_HPG * SSD_HEADDIM
    b_ts, cbs, y_offs = [], [], []
    for g in range(SSD_GROUPS):
        b_g = xc_ref[D_SSM // LANES + g]
        c_g = xc_ref[(D_SSM + SSD_BC) // LANES + g].astype(BF16)
        b_t = b_g.T
        b_ts.append(b_t)
        cbs.append(jnp.dot(c_g, b_t.astype(BF16), preferred_element_type=F32))
        y_offs.append(jnp.dot(c_g, state_ref[g].astype(BF16), preferred_element_type=F32))
    pairs = [(g, pair) for g in range(SSD_GROUPS) for pair in range(SSD_HPG // 2)]
    pair_cols = lambda g, pair: slice(g * gw + pair * LANES, g * gw + (pair + 1) * LANES)
    pair_slab = lambda g, pair: g * (SSD_HPG // 2) + pair
    lhs_all, e_cs_all = [], []
    for g, pair in pairs:
        h0 = g * SSD_HPG + 2 * pair
        lhs_top, lhs_bot, e_cs = [], [], []
        for h in (h0, h0 + 1):
            cs_col = jnp.broadcast_to(cs[:, h:h + 1], (BLOCK, BLOCK))
            seg = cs_col - cs_t[h:h + 1, :]
            decay = jnp.exp(jnp.where(tril, seg, -jnp.inf))
            lhs_top.append((cbs[g] * decay * dt_t[h:h + 1, :]).astype(BF16))
            lhs_bot.append((b_ts[g] * dtw_t[h:h + 1, :]).astype(BF16))
            e_cs.append(jnp.exp(cs_col))
        lhs_all.append(jnp.concatenate([jnp.concatenate(lhs_top, axis=1), jnp.concatenate(lhs_bot, axis=1)], axis=0))
        e_cs_all.append(jnp.where(left, e_cs[0], e_cs[1]))
    res_all = []
    for (g, pair), lhs in zip(pairs, lhs_all):
        xp = xc_ref[pair_slab(g, pair)]
        x_bd = jnp.concatenate([jnp.where(left, xp, 0.0), jnp.where(left, 0.0, xp)], axis=0).astype(BF16)
        res_all.append(jnp.dot(lhs, x_bd, preferred_element_type=F32))
    for g in range(SSD_GROUPS):
        y_parts = []
        for pair in range(SSD_HPG // 2):
            h0 = g * SSD_HPG + 2 * pair
            idx = g * (SSD_HPG // 2) + pair
            res = res_all[idx]
            pcols = pair_cols(g, pair)
            scols = slice(pair * LANES, (pair + 1) * LANES)
            y_pair = res[:BLOCK] + y_offs[g][:, scols] * e_cs_all[idx]
            dec = jnp.where(left, chunk_decay[:, h0:h0 + 1], chunk_decay[:, h0 + 1:h0 + 2])
            state_ref[g, :, scols] = state_ref[g, :, scols] * dec + res[BLOCK:]
            y_parts.append(y_pair + dskip_ref[:, pcols] * xc_ref[pair_slab(g, pair)])
        gcols = slice(g * gw, (g + 1) * gw)
        y_g = jnp.concatenate(y_parts, axis=1) * zs_ref[:, gcols]
        y_g = y_g * lax.rsqrt(jnp.mean(y_g * y_g, axis=-1, keepdims=True) + EPS)
        o_ref[:, gcols] = (y_g * gn_ref[:, gcols]).astype(o_ref.dtype)


def _ssd(zs, xc, dt, a_log, d_skip, gate_norm):
    rowblk = lambda w: pl.BlockSpec((BLOCK, w), lambda i: (i, 0))
    gw = SSD_HPG * SSD_HEADDIM
    return pl.pallas_call(
        _ssd_kernel,
        grid=(ROWS // BLOCK,),
        in_specs=[rowblk(D_SSM), pl.BlockSpec((SSD_CONV_DIM // LANES, BLOCK, LANES), lambda i: (0, i, 0)), rowblk(LANES),
                  _const_spec((1, LANES)),
                  _const_spec((1, D_SSM)),
                  _const_spec((1, D_SSM))],
        out_specs=rowblk(D_SSM),
        out_shape=jax.ShapeDtypeStruct((ROWS, D_SSM), BF16),
        scratch_shapes=[pltpu.VMEM((SSD_GROUPS, SSD_STATE, gw), F32)],
        compiler_params=_params(),
        name="l1_ssd",
    )(zs, xc, dt, jnp.pad(a_log, (0, LANES - SSD_HEADS)).reshape(1, LANES),
      jnp.repeat(d_skip, SSD_HEADDIM).reshape(1, D_SSM), gate_norm.reshape(1, D_SSM))


def kernel(x, meta_tokens,
           l0_mix_pre_norm, l0_mix_post_norm, l0_w_in, l0_lru_conv_w, l0_lru_conv_b,
           l0_lru_w_a, l0_lru_b_a, l0_lru_w_x, l0_lru_b_x, l0_lru_lambda, l0_attn_sinks, l0_w_out,
           l0_ffn_pre_norm, l0_ffn_post_norm, l0_ffn_w_up, l0_ffn_conv_w, l0_ffn_conv_b, l0_ffn_w_down,
           l1_mix_pre_norm, l1_mix_post_norm, l1_w_in, l1_ssm_conv_w, l1_ssm_conv_b,
           l1_dt_bias, l1_a_log, l1_d_skip, l1_gate_norm, l1_w_out,
           l1_ffn_pre_norm, l1_ffn_post_norm, l1_ffn_w_up, l1_ffn_conv_w, l1_ffn_conv_b, l1_ffn_w_down):
    bsz = x.shape[0]
    x2d = x.reshape(bsz * SEQ, D_MODEL)
    meta = meta_tokens.astype(x.dtype)

    kq = 2 * D_RNN + Q_DIM
    dup = lambda w: jnp.concatenate([w[:, :HEAD_DIM]] * 2 + [w[:, HEAD_DIM:]] * 2, axis=1)
    w0 = jnp.concatenate([l0_w_in[:, :kq], dup(l0_w_in[:, kq:kq + KV_DIM]), dup(l0_w_in[:, kq + KV_DIM:])],
                         axis=1).astype(BF16)
    gg, a, u, qkv = _l0_inproj(x2d, meta, l0_mix_pre_norm, w0, l0_lru_conv_w, l0_lru_conv_b, l0_lru_w_a, l0_lru_b_a,
                               l0_lru_w_x, l0_lru_b_x, l0_lru_lambda, 640)
    y_a = _lru(gg, a, u, 5 * BLOCK)
    y_b = _attention(qkv, l0_attn_sinks)
    h = _outproj([y_a, y_b], l0_w_out.astype(BF16), (x2d, meta), l0_mix_post_norm, 640, "l0_outproj")
    h = _ffn(h, l0_ffn_pre_norm, l0_ffn_w_up, l0_ffn_conv_w, l0_ffn_conv_b, l0_ffn_w_down, l0_ffn_post_norm,
             640, "l0_ffn")

    w1 = jnp.pad(l1_w_in, ((0, 0), (0, LANES - SSD_HEADS))).astype(BF16)
    zs, xc, dt = _l1_inproj(h, l1_mix_pre_norm, w1, l1_ssm_conv_w, l1_ssm_conv_b, l1_dt_bias, 416)
    y = _ssd(zs, xc, dt, l1_a_log, l1_d_skip, l1_gate_norm)
    h = _outproj([y], l1_w_out.astype(BF16), h, l1_mix_post_norm, 640, "l1_outproj")
    h = _ffn(h, l1_ffn_pre_norm, l1_ffn_w_up, l1_ffn_conv_w, l1_ffn_conv_b, l1_ffn_w_down, l1_ffn_post_norm,
             640, "l1_ffn")
    return h.reshape(bsz, LP, D_MODEL)[:, BLOCK:]
```

```python
import functools
import math

import jax
import jax.numpy as jnp
from jax import lax
from jax.experimental import pallas as pl
from jax.experimental.pallas import tpu as pltpu

D_MODEL = 1024
BATCH = 2
SEQ = 8192
N_META = 16
BLOCK = 128
PAD = BLOCK - N_META
LP = PAD + N_META + SEQ
NBLK = LP // BLOCK
ROWS = BATCH * LP

D_RNN = D_MODEL
LRU_BLOCKS = 8
LRU_BS = D_RNN // LRU_BLOCKS
LRU_C = 8.0
CONV_A = 4

N_Q_HEADS = 16
N_KV_HEADS = 2
HEAD_DIM = 64
Q_PER_KV = N_Q_HEADS // N_KV_HEADS
WINDOW = 128
Q_DIM = N_Q_HEADS * HEAD_DIM
KV_DIM = N_KV_HEADS * HEAD_DIM

D_SSM = 2 * D_MODEL
SSD_HEADDIM = 64
SSD_HEADS = D_SSM // SSD_HEADDIM
SSD_GROUPS = 8
SSD_HPG = SSD_HEADS // SSD_GROUPS
SSD_STATE = 128
CONV_C = 4
SSD_BC = SSD_GROUPS * SSD_STATE
SSD_CONV_DIM = D_SSM + 2 * SSD_BC

D_FF = 2816
CONV_F = 3
FF_CHUNK = 256
N_FF_CHUNKS = D_FF // FF_CHUNK

EPS = 1e-6
LANES = 128
SUBLANES = 8

VMEM_LIMIT = 56 * 1024 * 1024

F32 = jnp.float32
BF16 = jnp.bfloat16


def _rms(x, w):
    return x * lax.rsqrt(jnp.mean(x * x, axis=-1, keepdims=True) + EPS) * w


def _softplus(y):
    return jnp.maximum(y, 0.0) + jnp.log1p(jnp.exp(-jnp.abs(y)))


def _shift_rows(x, d, fill):
    n, c = x.shape
    if d % SUBLANES == 0:
        return jnp.concatenate([jnp.full((d, c), fill, x.dtype), x[: n - d]], axis=0)
    row = lax.broadcasted_iota(jnp.int32, x.shape, 0)
    return jnp.where(row >= d, pltpu.roll(x, d, 0), fill)


def _with_history(carry, x, k):
    xx = jnp.concatenate([carry, x], axis=0)
    return pltpu.roll(xx, k, 0)[SUBLANES:]


def _row_valid(tile_in_batch, tm):
    row = tile_in_batch * tm + lax.broadcasted_iota(jnp.int32, (tm, 1), 0)
    return row >= PAD


def _const_spec(shape):
    nd = len(shape)
    return pl.BlockSpec(shape, lambda i: (0,) * nd, pipeline_mode=pl.Buffered(1))


def _params():
    return pltpu.CompilerParams(dimension_semantics=("arbitrary",), vmem_limit_bytes=VMEM_LIMIT)


def _causal_conv(x, carry, cw, cb):
    taps = cw.shape[0]
    y = cb + cw[taps - 1:taps] * x
    for k in range(1, taps):
        y = y + cw[taps - 1 - k:taps - k] * _with_history(carry, x, k)
    return y


L0_IN_COLS = 2 * D_RNN + Q_DIM + 4 * KV_DIM
LRU_CHUNK = 2 * LRU_BS


def _token_block_spec(k, tm, pad_block_target):
    bpt, tpb, xb = tm // BLOCK, LP // tm, SEQ // BLOCK

    def index_map(i):
        n = (i % tpb) * bpt + k
        return ((i // tpb) * xb + jnp.where(n == 0, pad_block_target, n - 1), 0)

    return pl.BlockSpec((BLOCK, D_MODEL), index_map)


def _initial_stream_tile(tile, x_refs, meta_ref):
    first = jnp.concatenate([jnp.zeros((PAD, D_MODEL), F32), meta_ref[...]], axis=0)
    return jnp.concatenate([jnp.where(tile == 0, first, x_refs[0][...])] + [r[...] for r in x_refs[1:]], axis=0)


def _l0_inproj_kernel(*refs, tm, tpb):
    nx = tm // BLOCK
    x_refs = refs[:nx]
    (meta_ref, nw_ref, w_ref, cw_ref, cb_ref, wg_ref, ba_ref, bx_ref, lam_ref,
     gg_ref, a_ref, u_ref, qkv_ref, cx_ref) = refs[nx:]
    tile = pl.program_id(0) % tpb

    @pl.when(tile == 0)
    def _():
        cx_ref[...] = jnp.zeros_like(cx_ref)

    u_in = _rms(_initial_stream_tile(tile, x_refs, meta_ref), nw_ref[...]).astype(BF16)
    valid = _row_valid(tile, tm)
    xs = [jnp.dot(u_in, w_ref[:, D_RNN + c * LRU_CHUNK:D_RNN + (c + 1) * LRU_CHUNK], preferred_element_type=F32)
          for c in range(D_RNN // LRU_CHUNK)]
    gate = jnp.dot(u_in, w_ref[:, :D_RNN], preferred_element_type=F32)
    xcs = []
    for c, x in enumerate(xs):
        cols = slice(c * LRU_CHUNK, (c + 1) * LRU_CHUNK)
        xcs.append(_causal_conv(x, cx_ref[:, cols], cw_ref[:, cols], cb_ref[:, cols]))
        cx_ref[:, cols] = x[tm - SUBLANES:]
    pres = [jnp.dot(xcs[n // 2][:, (n % 2) * LRU_BS:(n % 2 + 1) * LRU_BS].astype(BF16), wg_ref[n],
                    preferred_element_type=F32) for n in range(LRU_BLOCKS)]
    qkv_ref[...] = jnp.dot(u_in, w_ref[:, 2 * D_RNN:], preferred_element_type=F32).astype(qkv_ref.dtype)
    for n in range(LRU_BLOCKS):
        cols = slice(n * LRU_BS, (n + 1) * LRU_BS)
        xc = xcs[n // 2][:, (n % 2) * LRU_BS:(n % 2 + 1) * LRU_BS]
        r = jax.nn.sigmoid(pres[n][:, :LRU_BS] + ba_ref[:, cols])
        i = jax.nn.sigmoid(pres[n][:, LRU_BS:] + bx_ref[:, cols])
        log_a = (-LRU_C * _softplus(-lam_ref[:, cols])) * r
        a = jnp.exp(log_a)
        a_ref[:, cols] = a
        u_ref[:, cols] = jnp.where(valid, jnp.sqrt(1.0 - a * a) * (i * xc), 0.0)
    gg_ref[...] = jax.nn.gelu(gate, approximate=True)


def _l0_inproj(x2d, meta, norm_w, w, conv_w, conv_b, w_a, b_a, w_x, b_x, lam, tm):
    wg = jnp.concatenate([w_a, w_x], axis=-1).astype(BF16)
    row = lambda v: v.reshape(1, D_RNN)
    rowblk = lambda width: pl.BlockSpec((tm, width), lambda i: (i, 0))
    nx = tm // BLOCK
    return pl.pallas_call(
        functools.partial(_l0_inproj_kernel, tm=tm, tpb=LP // tm),
        grid=(ROWS // tm,),
        in_specs=[_token_block_spec(k, tm, 0) for k in range(nx)]
                 + [_const_spec((N_META, D_MODEL)),
                  _const_spec((1, D_MODEL)),
                  _const_spec((D_MODEL, L0_IN_COLS)),
                  _const_spec((CONV_A, D_RNN)),
                  _const_spec((1, D_RNN)),
                  _const_spec((LRU_BLOCKS, LRU_BS, 2 * LRU_BS)),
                  _const_spec((1, D_RNN)),
                  _const_spec((1, D_RNN)),
                  _const_spec((1, D_RNN))],
        out_specs=[rowblk(D_RNN), rowblk(D_RNN), rowblk(D_RNN), rowblk(Q_DIM + 4 * KV_DIM)],
        out_shape=[jax.ShapeDtypeStruct((ROWS, D_RNN), F32)] * 3
                  + [jax.ShapeDtypeStruct((ROWS, Q_DIM + 4 * KV_DIM), BF16)],
        scratch_shapes=[pltpu.VMEM((SUBLANES, D_RNN), F32)],
        compiler_params=_params(),
        name="l0_inproj",
    )(*([x2d] * nx), meta, norm_w.reshape(1, D_MODEL), w, conv_w, row(conv_b), wg, row(b_a), row(b_x), row(lam))


L1_IN_COLS = D_SSM + SSD_CONV_DIM + LANES
SSD_CONV_CHUNK = 4 * LANES


def _l1_inproj_kernel(h_ref, nw_ref, w_ref, cw_ref, cb_ref, dtb_ref, zs_ref, xc_ref, dt_ref, *xs_refs, tm, tpb):
    tile = pl.program_id(0) % tpb
    q = tm // SUBLANES
    spc = SSD_CONV_CHUNK // LANES
    n_chunks = SSD_CONV_DIM // SSD_CONV_CHUNK

    @pl.when(tile == 0)
    def _():
        for xs_ref in xs_refs:
            xs_ref[:, 0:SUBLANES, :] = jnp.zeros((spc, SUBLANES, LANES), F32)

    u_in = _rms(h_ref[...], nw_ref[...]).astype(BF16)
    sub = lax.broadcasted_iota(jnp.int32, (SUBLANES, LANES), 0)
    row0 = tile * tm + sub * q
    valid_p = [row0 + v >= PAD for v in range(q)]
    zw = D_SSM // n_chunks
    for c in range(n_chunks):
        xs_ref = xs_refs[c]
        x = jnp.dot(u_in, w_ref[:, D_SSM + c * SSD_CONV_CHUNK:D_SSM + (c + 1) * SSD_CONV_CHUNK],
                    preferred_element_type=F32)
        for j in range(spc):
            xs_ref[j, SUBLANES:SUBLANES + tm, :] = x[:, j * LANES:(j + 1) * LANES]
        zcols = slice(c * zw, (c + 1) * zw)
        zs_ref[:, zcols] = jax.nn.silu(jnp.dot(u_in, w_ref[:, zcols], preferred_element_type=F32))
        for j in range(spc):
            lcols = slice(c * SSD_CONV_CHUNK + j * LANES, c * SSD_CONV_CHUNK + (j + 1) * LANES)
            wk = [jnp.broadcast_to(cw_ref[k:k + 1, lcols], (SUBLANES, LANES)) for k in range(CONV_C)]
            bias = jnp.broadcast_to(cb_ref[:, lcols], (SUBLANES, LANES))
            load = lambda start: xs_ref[j, pl.ds(start, SUBLANES, stride=q), :]
            window = [load(SUBLANES - k) for k in range(CONV_C - 1, 0, -1)]
            for v in range(q):
                window.append(load(SUBLANES + v))
                acc = bias
                for k in range(CONV_C):
                    acc = acc + wk[k] * window[k]
                window.pop(0)
                xc_ref[c * spc + j, pl.ds(v, SUBLANES, stride=q), :] = jnp.where(valid_p[v], jax.nn.silu(acc), 0.0)
        for j in range(spc):
            xs_ref[j, 0:SUBLANES, :] = xs_ref[j, tm:tm + SUBLANES, :]
    dt_raw = jnp.dot(u_in, w_ref[:, D_SSM + SSD_CONV_DIM:], preferred_element_type=F32)
    dt_ref[...] = jnp.where(_row_valid(tile, tm), _softplus(dt_raw + dtb_ref[...]), 0.0)


def _l1_inproj(h, norm_w, w, conv_w, conv_b, dt_bias, tm):
    rowblk = lambda width: pl.BlockSpec((tm, width), lambda i: (i, 0))
    n_slabs = SSD_CONV_DIM // LANES
    return pl.pallas_call(
        functools.partial(_l1_inproj_kernel, tm=tm, tpb=LP // tm),
        grid=(ROWS // tm,),
        in_specs=[rowblk(D_MODEL),
                  _const_spec((1, D_MODEL)),
                  _const_spec((D_MODEL, L1_IN_COLS)),
                  _const_spec((CONV_C, SSD_CONV_DIM)),
                  _const_spec((1, SSD_CONV_DIM)),
                  _const_spec((1, LANES))],
        out_specs=[rowblk(D_SSM), pl.BlockSpec((n_slabs, tm, LANES), lambda i: (0, i, 0)), rowblk(LANES)],
        out_shape=[jax.ShapeDtypeStruct((ROWS, D_SSM), F32),
                   jax.ShapeDtypeStruct((n_slabs, ROWS, LANES), F32),
                   jax.ShapeDtypeStruct((ROWS, LANES), F32)],
        scratch_shapes=[pltpu.VMEM((SSD_CONV_CHUNK // LANES, SUBLANES + tm, LANES), F32)
                        for _ in range(SSD_CONV_DIM // SSD_CONV_CHUNK)],
        compiler_params=_params(),
        name="l1_inproj",
    )(h, norm_w.reshape(1, D_MODEL), w, conv_w, conv_b.reshape(1, SSD_CONV_DIM),
      jnp.pad(dt_bias, (0, LANES - SSD_HEADS)).reshape(1, LANES))


def _outproj_kernel(*refs, widths, from_tokens, tm, tpb):
    a_refs = refs[:len(widths)]
    w_ref, pw_ref = refs[len(widths):len(widths) + 2]
    res_refs = refs[len(widths) + 2:-1]
    o_ref = refs[-1]
    tile = pl.program_id(0) % tpb
    y = None
    r0 = 0
    for a_ref, width in zip(a_refs, widths):
        part = jnp.dot(a_ref[...], w_ref[r0:r0 + width, :], preferred_element_type=F32)
        y = part if y is None else y + part
        r0 += width
    h = _initial_stream_tile(tile, res_refs[:-1], res_refs[-1]) if from_tokens else res_refs[0][...]
    o_ref[...] = jnp.where(_row_valid(tile, tm), h + _rms(y, pw_ref[...]), 0.0)


def _outproj(acts, w, residual, post_w, tm, name):
    widths = tuple(a.shape[1] for a in acts)
    from_tokens = isinstance(residual, tuple)
    if from_tokens:
        nx = tm // BLOCK
        res_specs = [_token_block_spec(k, tm, 0) for k in range(nx)] + [_const_spec((N_META, D_MODEL))]
        res_args = [residual[0]] * nx + [residual[1]]
    else:
        res_specs = [pl.BlockSpec((tm, D_MODEL), lambda i: (i, 0))]
        res_args = [residual]
    return pl.pallas_call(
        functools.partial(_outproj_kernel, widths=widths, from_tokens=from_tokens, tm=tm, tpb=LP // tm),
        grid=(ROWS // tm,),
        in_specs=[pl.BlockSpec((tm, k), lambda i: (i, 0)) for k in widths]
                 + [_const_spec((sum(widths), D_MODEL)), _const_spec((1, D_MODEL))] + res_specs,
        out_specs=pl.BlockSpec((tm, D_MODEL), lambda i: (i, 0)),
        out_shape=jax.ShapeDtypeStruct((ROWS, D_MODEL), F32),
        compiler_params=_params(),
        name=name,
    )(*acts, w, post_w.reshape(1, D_MODEL), *res_args)


def _ffn_kernel(*refs, tm, tpb, tokens_only):
    nh = tm // BLOCK if tokens_only else 1
    h_refs, meta_rows_ref = refs[:nh], (refs[nh] if tokens_only else None)
    nw_ref, wup_ref, cw_ref, cb_ref, wdn_ref, pw_ref, o_ref, carry_ref, act_ref = refs[nh + int(tokens_only):]
    tile = pl.program_id(0) % tpb

    @pl.when(tile == 0)
    def _():
        if tokens_only:
            um = _rms(meta_rows_ref[...], nw_ref[...]).astype(BF16)
            xm = jnp.dot(um, wup_ref[...], preferred_element_type=F32)
            carry_ref[...] = xm[N_META - SUBLANES:]
        else:
            carry_ref[...] = jnp.zeros_like(carry_ref)

    h = jnp.concatenate([r[...] for r in h_refs], axis=0) if nh > 1 else h_refs[0][...]
    u = _rms(h, nw_ref[...]).astype(BF16)

    def conv_chunk(c0):
        cols = slice(c0, c0 + FF_CHUNK)
        x = jnp.dot(u, wup_ref[:, cols], preferred_element_type=F32)
        carry = carry_ref[:, cols]
        x1 = _with_history(carry, x, 1)
        x2 = _with_history(carry, x, 2)
        carry_ref[:, cols] = x[tm - SUBLANES:]
        w = cw_ref[:, cols]
        return cb_ref[:, cols] + w[2:3] * x + w[1:2] * x1 + w[0:1] * x2

    for j in range(N_FF_CHUNKS):
        g = conv_chunk(j * FF_CHUNK)
        v = conv_chunk(D_FF + j * FF_CHUNK)
        act_ref[:, j * FF_CHUNK:(j + 1) * FF_CHUNK] = (jax.nn.gelu(g, approximate=True) * v).astype(BF16)

    y = jnp.dot(act_ref[...], wdn_ref[...], preferred_element_type=F32)
    o = h + _rms(y, pw_ref[...])
    o_ref[...] = o if tokens_only else jnp.where(_row_valid(tile, tm), o, 0.0)


def _ffn(h, pre_w, w_up, conv_w, conv_b, w_down, post_w, tm, name, tokens_only=False):
    if tokens_only:
        tpb, n_rows = SEQ // tm, BATCH * SEQ
        h_specs = [pl.BlockSpec((BLOCK, D_MODEL),
                                lambda i, k=k: ((i // tpb) * NBLK + 1 + (i % tpb) * (tm // BLOCK) + k, 0))
                   for k in range(tm // BLOCK)]
        h_specs.append(pl.BlockSpec((N_META, D_MODEL), lambda i: ((i // tpb) * (LP // N_META) + PAD // N_META, 0)))
        h_args = [h] * (tm // BLOCK + 1)
    else:
        tpb, n_rows = LP // tm, ROWS
        h_specs = [pl.BlockSpec((tm, D_MODEL), lambda i: (i, 0))]
        h_args = [h]
    return pl.pallas_call(
        functools.partial(_ffn_kernel, tm=tm, tpb=tpb, tokens_only=tokens_only),
        grid=(n_rows // tm,),
        in_specs=h_specs + [_const_spec((1, D_MODEL)),
                            _const_spec((D_MODEL, 2 * D_FF)),
                            _const_spec((CONV_F, 2 * D_FF)),
                            _const_spec((1, 2 * D_FF)),
                            _const_spec((D_FF, D_MODEL)),
                            _const_spec((1, D_MODEL))],
        out_specs=pl.BlockSpec((tm, D_MODEL), lambda i: (i, 0)),
        out_shape=jax.ShapeDtypeStruct((n_rows, D_MODEL), F32),
        scratch_shapes=[pltpu.VMEM((SUBLANES, 2 * D_FF), F32),
                        pltpu.VMEM((tm, D_FF), BF16)],
        compiler_params=_params(),
        name=name,
    )(*h_args, pre_w.reshape(1, D_MODEL), w_up.astype(BF16), conv_w, conv_b.reshape(1, 2 * D_FF),
      w_down.astype(BF16), post_w.reshape(1, D_MODEL))


def _linear_scan(a, u):
    n = a.shape[0]
    d = 1
    while d < n:
        a_s = _shift_rows(a, d, 1.0)
        u_s = _shift_rows(u, d, 0.0)
        u = a * u_s + u
        a = a * a_s
        d *= 2
    return a, u


def _lru_kernel(gg_ref, a_ref, u_ref, o_ref, ch_ref, *, tr):
    tile = pl.program_id(0) % (LP // tr)

    @pl.when(tile == 0)
    def _():
        ch_ref[...] = jnp.zeros_like(ch_ref)

    for n in range(LRU_BLOCKS):
        cols = slice(n * LRU_BS, (n + 1) * LRU_BS)
        carry = ch_ref[SUBLANES - 1:SUBLANES, cols]
        for r0 in range(0, tr, BLOCK):
            rows = slice(r0, r0 + BLOCK)
            a_cum, hloc = _linear_scan(a_ref[rows, cols], u_ref[rows, cols])
            h = hloc + a_cum * carry
            carry = h[BLOCK - 1:BLOCK]
            o_ref[rows, cols] = (gg_ref[rows, cols] * h).astype(o_ref.dtype)
        ch_ref[:, cols] = jnp.broadcast_to(carry, (SUBLANES, LRU_BS))


def _lru(gg, a, u, tr):
    rowblk = pl.BlockSpec((tr, D_RNN), lambda i: (i, 0))
    return pl.pallas_call(
        functools.partial(_lru_kernel, tr=tr),
        grid=(ROWS // tr,),
        in_specs=[rowblk, rowblk, rowblk],
        out_specs=rowblk,
        out_shape=jax.ShapeDtypeStruct((ROWS, D_RNN), BF16),
        scratch_shapes=[pltpu.VMEM((SUBLANES, D_RNN), F32)],
        compiler_params=_params(),
        name="l0_lru",
    )(gg, a, u)


def _attn_kernel(sink_ref, q_ref, km_ref, kp_ref, kc_ref, vm_ref, vp_ref, vc_ref, o_ref):
    n = pl.program_id(0) % NBLK
    jrow = lax.broadcasted_iota(jnp.int32, (BLOCK, BLOCK), 0)
    icol = lax.broadcasted_iota(jnp.int32, (BLOCK, BLOCK), 1)
    from_prev = jrow > icol
    neg_band = jnp.where(from_prev,
                         jnp.where(n >= 2, -(BLOCK + icol - jrow).astype(F32), -jnp.inf),
                         jnp.where(n >= 1, -(icol - jrow).astype(F32), -jnp.inf))
    m_row = lax.broadcasted_iota(jnp.int32, (N_META, BLOCK), 0)
    i_meta = lax.broadcasted_iota(jnp.int32, (N_META, BLOCK), 1)
    d_meta = n * BLOCK + i_meta - PAD - m_row
    neg_meta = jnp.where(d_meta >= 0, -jnp.minimum(d_meta, WINDOW).astype(F32), -jnp.inf)

    lane = lax.broadcasted_iota(jnp.int32, (1, LANES), 1)
    left = lane < HEAD_DIM
    halves = lambda x: (jnp.where(left, x, 0.0).astype(BF16), jnp.where(left, 0.0, x).astype(BF16))
    pairs_per_group = Q_PER_KV // 2
    scores, v_stacks = [], []
    for g in range(N_KV_HEADS):
        gcols = slice(g * LANES, (g + 1) * LANES)
        k_all = jnp.concatenate([kp_ref[:, gcols], kc_ref[:, gcols], km_ref[PAD:, gcols]], axis=0)
        k_all = k_all * (HEAD_DIM ** -0.5)
        v_left, v_right = halves(jnp.concatenate([vp_ref[:, gcols], vc_ref[:, gcols], vm_ref[PAD:, gcols]], axis=0))
        v_stacks.append(jnp.concatenate([v_left, v_right], axis=0))
        for pair in range(pairs_per_group):
            pcols = slice((g * pairs_per_group + pair) * LANES, (g * pairs_per_group + pair + 1) * LANES)
            q_left, q_right = halves(q_ref[:, pcols])
            q_both = jnp.concatenate([q_left, q_right], axis=0)
            scores.append(lax.dot_general(k_all, q_both, (((1,), (1,)), ((), ())), preferred_element_type=F32))
    for g in range(N_KV_HEADS):
        v_both = v_stacks[g]
        for pair in range(pairs_per_group):
            pcols = slice((g * pairs_per_group + pair) * LANES, (g * pairs_per_group + pair + 1) * LANES)
            s_t = scores[g * pairs_per_group + pair]
            p_parts = []
            for half in range(2):
                head = g * Q_PER_KV + 2 * pair + half
                slope = 2.0 ** (-8.0 * (head + 1) / N_Q_HEADS)
                sink = sink_ref[head]
                cols = slice(half * BLOCK, (half + 1) * BLOCK)
                band = jnp.where(from_prev, s_t[:BLOCK, cols], s_t[BLOCK:2 * BLOCK, cols]) + slope * neg_band
                meta = s_t[2 * BLOCK:, cols] + slope * neg_meta
                mx = jnp.maximum(jnp.maximum(jnp.max(band, axis=0, keepdims=True),
                                             jnp.max(meta, axis=0, keepdims=True)), sink)
                e_band = jnp.exp(band - mx)
                e_meta = jnp.exp(meta - mx)
                denom = (jnp.sum(e_band, axis=0, keepdims=True) + jnp.sum(e_meta, axis=0, keepdims=True)
                         + jnp.exp(sink - mx))
                r = 1.0 / denom
                p_band = e_band * r
                p_parts += [jnp.where(from_prev, p_band, 0.0).astype(BF16),
                            jnp.where(from_prev, 0.0, p_band).astype(BF16),
                            (e_meta * r).astype(BF16)]
            p_t = jnp.concatenate(p_parts, axis=0)
            o_pair = lax.dot_general(p_t, v_both, (((0,), (0,)), ((), ())), preferred_element_type=F32)
            o_ref[:, pcols] = o_pair.astype(o_ref.dtype)


def _attention(qkv, sinks):
    kcol = Q_DIM // (2 * LANES)
    vcol = kcol + 1
    first = lambda i: (i // NBLK) * NBLK
    prev = lambda i: jnp.maximum(i - 1, 0)
    kv = lambda rowf, col: pl.BlockSpec((BLOCK, 2 * LANES), lambda i: (rowf(i), col))
    return pl.pallas_call(
        _attn_kernel,
        grid=(ROWS // BLOCK,),
        in_specs=[pl.BlockSpec(memory_space=pltpu.SMEM),
                  pl.BlockSpec((BLOCK, Q_DIM), lambda i: (i, 0)),
                  kv(first, kcol), kv(prev, kcol), kv(lambda i: i, kcol),
                  kv(first, vcol), kv(prev, vcol), kv(lambda i: i, vcol)],
        out_specs=pl.BlockSpec((BLOCK, Q_DIM), lambda i: (i, 0)),
        out_shape=jax.ShapeDtypeStruct((ROWS, Q_DIM), BF16),
        compiler_params=_params(),
        name="l0_attn",
    )(sinks, qkv, qkv, qkv, qkv, qkv, qkv, qkv)


def _ssd_kernel(zs_ref, xc_ref, dt_ref, alog_ref, dskip_ref, gn_ref, o_ref, state_ref):
    n = pl.program_id(0) % NBLK

    @pl.when(n == 0)
    def _():
        state_ref[...] = jnp.zeros_like(state_ref)

    dt = dt_ref[...]
    a = -jnp.exp(alog_ref[...])
    cs = dt * a
    d = 1
    while d < BLOCK:
        cs = cs + _shift_rows(cs, d, 0.0)
        d *= 2
    cs_last = cs[BLOCK - 1:BLOCK, :]
    dtw = dt * jnp.exp(cs_last - cs)
    chunk_decay = jnp.exp(cs_last)
    e_cs_lanes = jnp.exp(cs)
    cs_mlog_t = (cs - jnp.log(dt)).T
    dtw_t = dtw.T

    row = lax.broadcasted_iota(jnp.int32, (BLOCK, BLOCK), 0)
    col = lax.broadcasted_iota(jnp.int32, (BLOCK, BLOCK), 1)
    tril = row >= col
    lane = lax.broadcasted_iota(jnp.int32, (1, LANES), 1)
    left = lane < SSD_HEADDIM

    gw = SSD_HPG * SSD_HEADDIM
    b_ts, cbs, y_offs = [], [], []
    for g in range(SSD_GROUPS):
        b_g = xc_ref[D_SSM // LANES + g]
        c_g = xc_ref[(D_SSM + SSD_BC) // LANES + g].astype(BF16)
        b_t = b_g.T
        b_ts.append(b_t)
        cbs.append(jnp.dot(c_g, b_t.astype(BF16), preferred_element_type=F32))
        y_offs.append(jnp.dot(c_g, state_ref[g].astype(BF16), preferred_element_type=F32))
    pairs = [(g, pair) for g in range(SSD_GROUPS) for pair in range(SSD_HPG // 2)]
    pair_cols = lambda g, pair: slice(g * gw + pair * LANES, g * gw + (pair + 1) * LANES)
    pair_slab = lambda g, pair: g * (SSD_HPG // 2) + pair
    lhs_all, e_cs_all = [], []
    for g, pair in pairs:
        h0 = g * SSD_HPG + 2 * pair
        lhs_top, lhs_bot, e_cs = [], [], []
        for h in (h0, h0 + 1):
            cs_col = jnp.broadcast_to(cs[:, h:h + 1], (BLOCK, BLOCK))
            seg = cs_col - cs_mlog_t[h:h + 1, :]
            decay_dt = jnp.exp(jnp.where(tril, seg, -jnp.inf))
            lhs_top.append((cbs[g] * decay_dt).astype(BF16))
            lhs_bot.append((b_ts[g] * dtw_t[h:h + 1, :]).astype(BF16))
            e_cs.append(jnp.broadcast_to(e_cs_lanes[:, h:h + 1], (BLOCK, BLOCK)))
        lhs_all.append(jnp.concatenate([jnp.concatenate(lhs_top, axis=1), jnp.concatenate(lhs_bot, axis=1)], axis=0))
        e_cs_all.append(jnp.where(left, e_cs[0], e_cs[1]))
    res_all = []
    for (g, pair), lhs in zip(pairs, lhs_all):
        xp = xc_ref[pair_slab(g, pair)]
        x_bd = jnp.concatenate([jnp.where(left, xp, 0.0), jnp.where(left, 0.0, xp)], axis=0).astype(BF16)
        res_all.append(jnp.dot(lhs, x_bd, preferred_element_type=F32))
    for g in range(SSD_GROUPS):
        y_parts = []
        for pair in range(SSD_HPG // 2):
            h0 = g * SSD_HPG + 2 * pair
            idx = g * (SSD_HPG // 2) + pair
            res = res_all[idx]
            pcols = pair_cols(g, pair)
            scols = slice(pair * LANES, (pair + 1) * LANES)
            y_pair = res[:BLOCK] + y_offs[g][:, scols] * e_cs_all[idx]
            dec = jnp.where(left, chunk_decay[:, h0:h0 + 1], chunk_decay[:, h0 + 1:h0 + 2])
            state_ref[g, :, scols] = state_ref[g, :, scols] * dec + res[BLOCK:]
            y_parts.append(y_pair + dskip_ref[:, pcols] * xc_ref[pair_slab(g, pair)])
        gcols = slice(g * gw, (g + 1) * gw)
        y_g = jnp.concatenate(y_parts, axis=1) * zs_ref[:, gcols]
        y_g = y_g * lax.rsqrt(jnp.mean(y_g * y_g, axis=-1, keepdims=True) + EPS)
        o_ref[:, gcols] = (y_g * gn_ref[:, gcols]).astype(o_ref.dtype)


def _ssd(zs, xc, dt, a_log, d_skip, gate_norm):
    rowblk = lambda w: pl.BlockSpec((BLOCK, w), lambda i: (i, 0))
    gw = SSD_HPG * SSD_HEADDIM
    return pl.pallas_call(
        _ssd_kernel,
        grid=(ROWS // BLOCK,),
        in_specs=[rowblk(D_SSM), pl.BlockSpec((SSD_CONV_DIM // LANES, BLOCK, LANES), lambda i: (0, i, 0)), rowblk(LANES),
                  _const_spec((1, LANES)),
                  _const_spec((1, D_SSM)),
                  _const_spec((1, D_SSM))],
        out_specs=rowblk(D_SSM),
        out_shape=jax.ShapeDtypeStruct((ROWS, D_SSM), BF16),
        scratch_shapes=[pltpu.VMEM((SSD_GROUPS, SSD_STATE, gw), F32)],
        compiler_params=_params(),
        name="l1_ssd",
    )(zs, xc, dt, jnp.pad(a_log, (0, LANES - SSD_HEADS)).reshape(1, LANES),
      jnp.repeat(d_skip, SSD_HEADDIM).reshape(1, D_SSM), gate_norm.reshape(1, D_SSM))


def kernel(x, meta_tokens,
           l0_mix_pre_norm, l0_mix_post_norm, l0_w_in, l0_lru_conv_w, l0_lru_conv_b,
           l0_lru_w_a, l0_lru_b_a, l0_lru_w_x, l0_lru_b_x, l0_lru_lambda, l0_attn_sinks, l0_w_out,
           l0_ffn_pre_norm, l0_ffn_post_norm, l0_ffn_w_up, l0_ffn_conv_w, l0_ffn_conv_b, l0_ffn_w_down,
           l1_mix_pre_norm, l1_mix_post_norm, l1_w_in, l1_ssm_conv_w, l1_ssm_conv_b,
           l1_dt_bias, l1_a_log, l1_d_skip, l1_gate_norm, l1_w_out,
           l1_ffn_pre_norm, l1_ffn_post_norm, l1_ffn_w_up, l1_ffn_conv_w, l1_ffn_conv_b, l1_ffn_w_down):
    bsz = x.shape[0]
    x2d = x.reshape(bsz * SEQ, D_MODEL)
    meta = meta_tokens.astype(x.dtype)

    kq = 2 * D_RNN + Q_DIM
    dup = lambda w: jnp.concatenate([w[:, :HEAD_DIM]] * 2 + [w[:, HEAD_DIM:]] * 2, axis=1)
    w0 = jnp.concatenate([l0_w_in[:, :kq], dup(l0_w_in[:, kq:kq + KV_DIM]), dup(l0_w_in[:, kq + KV_DIM:])],
                         axis=1).astype(BF16)
    gg, a, u, qkv = _l0_inproj(x2d, meta, l0_mix_pre_norm, w0, l0_lru_conv_w, l0_lru_conv_b, l0_lru_w_a, l0_lru_b_a,
                               l0_lru_w_x, l0_lru_b_x, l0_lru_lambda, 640)
    y_a = _lru(gg, a, u, 5 * BLOCK)
    y_b = _attention(qkv, l0_attn_sinks)
    h = _outproj([y_a, y_b], l0_w_out.astype(BF16), (x2d, meta), l0_mix_post_norm, 640, "l0_outproj")
    h = _ffn(h, l0_ffn_pre_norm, l0_ffn_w_up, l0_ffn_conv_w, l0_ffn_conv_b, l0_ffn_w_down, l0_ffn_post_norm,
             640, "l0_ffn")

    w1 = jnp.pad(l1_w_in, ((0, 0), (0, LANES - SSD_HEADS))).astype(BF16)
    zs, xc, dt = _l1_inproj(h, l1_mix_pre_norm, w1, l1_ssm_conv_w, l1_ssm_conv_b, l1_dt_bias, 416)
    y = _ssd(zs, xc, dt, l1_a_log, l1_d_skip, l1_gate_norm)
    h = _outproj([y], l1_w_out.astype(BF16), h, l1_mix_post_norm, 640, "l1_outproj")
    out = _ffn(h, l1_ffn_pre_norm, l1_ffn_w_up, l1_ffn_conv_w, l1_ffn_conv_b, l1_ffn_w_down, l1_ffn_post_norm,
               4 * BLOCK, "l1_ffn", tokens_only=True)
    return out.reshape(bsz, SEQ, D_MODEL)
```

```python
import functools
import math

import jax
import jax.numpy as jnp
from jax import lax
from jax.experimental import pallas as pl
from jax.experimental.pallas import tpu as pltpu

D_MODEL = 1024
BATCH = 2
SEQ = 8192
N_META = 16
BLOCK = 128
PAD = BLOCK - N_META
LP = PAD + N_META + SEQ
NBLK = LP // BLOCK
ROWS = BATCH * LP

D_RNN = D_MODEL
LRU_BLOCKS = 8
LRU_BS = D_RNN // LRU_BLOCKS
LRU_C = 8.0
CONV_A = 4

N_Q_HEADS = 16
N_KV_HEADS = 2
HEAD_DIM = 64
Q_PER_KV = N_Q_HEADS // N_KV_HEADS
WINDOW = 128
Q_DIM = N_Q_HEADS * HEAD_DIM
KV_DIM = N_KV_HEADS * HEAD_DIM

D_SSM = 2 * D_MODEL
SSD_HEADDIM = 64
SSD_HEADS = D_SSM // SSD_HEADDIM
SSD_GROUPS = 8
SSD_HPG = SSD_HEADS // SSD_GROUPS
SSD_STATE = 128
CONV_C = 4
SSD_BC = SSD_GROUPS * SSD_STATE
SSD_CONV_DIM = D_SSM + 2 * SSD_BC

D_FF = 2816
CONV_F = 3
FF_CHUNK = 256
N_FF_CHUNKS = D_FF // FF_CHUNK

EPS = 1e-6
LANES = 128
SUBLANES = 8

VMEM_LIMIT = 56 * 1024 * 1024

F32 = jnp.float32
BF16 = jnp.bfloat16


def _rms(x, w):
    return x * lax.rsqrt(jnp.mean(x * x, axis=-1, keepdims=True) + EPS) * w


_GELU_C0 = math.sqrt(2.0 / math.pi)
_GELU_C1 = _GELU_C0 * 0.044715


def _gelu_tanh_times(x, half_v):
    return (x * half_v) * (1.0 + jnp.tanh(x * (_GELU_C0 + _GELU_C1 * (x * x))))


def _sigmoid(x):
    return 0.5 + 0.5 * jnp.tanh(0.5 * x)


def _silu(x):
    hx = 0.5 * x
    return hx + hx * jnp.tanh(hx)


def _softplus(y):
    return jnp.maximum(y, 0.0) + jnp.log1p(jnp.exp(-jnp.abs(y)))


def _shift_rows(x, d, fill):
    n, c = x.shape
    if d % SUBLANES == 0:
        return jnp.concatenate([jnp.full((d, c), fill, x.dtype), x[: n - d]], axis=0)
    row = lax.broadcasted_iota(jnp.int32, x.shape, 0)
    return jnp.where(row >= d, pltpu.roll(x, d, 0), fill)


def _with_history(carry, x, k):
    xx = jnp.concatenate([carry, x], axis=0)
    return pltpu.roll(xx, k, 0)[SUBLANES:]


def _row_valid(tile_in_batch, tm):
    row = tile_in_batch * tm + lax.broadcasted_iota(jnp.int32, (tm, 1), 0)
    return row >= PAD


def _const_spec(shape):
    nd = len(shape)
    return pl.BlockSpec(shape, lambda i: (0,) * nd, pipeline_mode=pl.Buffered(1))


def _params():
    return pltpu.CompilerParams(dimension_semantics=("arbitrary",), vmem_limit_bytes=VMEM_LIMIT)


def _causal_conv(x, carry, cw, cb):
    taps = cw.shape[0]
    y = cb + cw[taps - 1:taps] * x
    for k in range(1, taps):
        y = y + cw[taps - 1 - k:taps - k] * _with_history(carry, x, k)
    return y


L0_IN_COLS = 2 * D_RNN + Q_DIM
LRU_CHUNK = 2 * LRU_BS


def _token_block_spec(k, tm, pad_block_target):
    bpt, tpb, xb = tm // BLOCK, LP // tm, SEQ // BLOCK

    def index_map(i):
        n = (i % tpb) * bpt + k
        return ((i // tpb) * xb + jnp.where(n == 0, pad_block_target, n - 1), 0)

    return pl.BlockSpec((BLOCK, D_MODEL), index_map)


def _initial_stream_tile(tile, x_refs, meta_ref):
    first = jnp.concatenate([jnp.zeros((PAD, D_MODEL), F32), meta_ref[...]], axis=0)
    return jnp.concatenate([jnp.where(tile == 0, first, x_refs[0][...])] + [r[...] for r in x_refs[1:]], axis=0)


def _l0_inproj_kernel(*refs, tm, tpb):
    nx = tm // BLOCK
    x_refs = refs[:nx]
    (meta_ref, nw_ref, w_ref, wkv_ref, cw_ref, cb_ref, wg_ref, ba_ref, bx_ref, lam_ref,
     gg_ref, a_ref, u_ref, qkv_ref, cx_ref) = refs[nx:]
    tile = pl.program_id(0) % tpb

    @pl.when(tile == 0)
    def _():
        cx_ref[...] = jnp.zeros_like(cx_ref)

    u_in = _rms(_initial_stream_tile(tile, x_refs, meta_ref), nw_ref[...]).astype(BF16)
    valid = _row_valid(tile, tm)
    xs = [jnp.dot(u_in, w_ref[:, D_RNN + c * LRU_CHUNK:D_RNN + (c + 1) * LRU_CHUNK], preferred_element_type=F32)
          for c in range(D_RNN // LRU_CHUNK)]
    gate = jnp.dot(u_in, w_ref[:, :D_RNN], preferred_element_type=F32)
    xcs = []
    for c, x in enumerate(xs):
        cols = slice(c * LRU_CHUNK, (c + 1) * LRU_CHUNK)
        xcs.append(_causal_conv(x, cx_ref[:, cols], cw_ref[:, cols], cb_ref[:, cols]))
        cx_ref[:, cols] = x[tm - SUBLANES:]
    pres = [jnp.dot(xcs[n // 2][:, (n % 2) * LRU_BS:(n % 2 + 1) * LRU_BS].astype(BF16), wg_ref[n],
                    preferred_element_type=F32) for n in range(LRU_BLOCKS)]
    qkv_ref[:, :Q_DIM] = jnp.dot(u_in, w_ref[:, 2 * D_RNN:], preferred_element_type=F32).astype(qkv_ref.dtype)
    qkv_ref[:, Q_DIM:] = jnp.dot(u_in, wkv_ref[...], preferred_element_type=F32).astype(qkv_ref.dtype)
    for n in range(LRU_BLOCKS):
        cols = slice(n * LRU_BS, (n + 1) * LRU_BS)
        xc = xcs[n // 2][:, (n % 2) * LRU_BS:(n % 2 + 1) * LRU_BS]
        r = _sigmoid(pres[n][:, :LRU_BS] + ba_ref[:, cols])
        i = _sigmoid(pres[n][:, LRU_BS:] + bx_ref[:, cols])
        log_a = (-LRU_C * _softplus(-lam_ref[:, cols])) * r
        a = jnp.exp(log_a)
        a_ref[:, cols] = a
        u_ref[:, cols] = jnp.where(valid, jnp.sqrt(1.0 - a * a) * (i * xc), 0.0)
    gg_ref[...] = _gelu_tanh_times(gate, 0.5)


def _l0_inproj(x2d, meta, norm_w, w, wkv, conv_w, conv_b, w_a, b_a, w_x, b_x, lam, tm):
    wg = jnp.concatenate([w_a, w_x], axis=-1).astype(BF16)
    row = lambda v: v.reshape(1, D_RNN)
    rowblk = lambda width: pl.BlockSpec((tm, width), lambda i: (i, 0))
    nx = tm // BLOCK
    return pl.pallas_call(
        functools.partial(_l0_inproj_kernel, tm=tm, tpb=LP // tm),
        grid=(ROWS // tm,),
        in_specs=[_token_block_spec(k, tm, 0) for k in range(nx)]
                 + [_const_spec((N_META, D_MODEL)),
                  _const_spec((1, D_MODEL)),
                  _const_spec((D_MODEL, L0_IN_COLS)),
                  _const_spec((D_MODEL, 4 * KV_DIM)),
                  _const_spec((CONV_A, D_RNN)),
                  _const_spec((1, D_RNN)),
                  _const_spec((LRU_BLOCKS, LRU_BS, 2 * LRU_BS)),
                  _const_spec((1, D_RNN)),
                  _const_spec((1, D_RNN)),
                  _const_spec((1, D_RNN))],
        out_specs=[rowblk(D_RNN), rowblk(D_RNN), rowblk(D_RNN), rowblk(Q_DIM + 4 * KV_DIM)],
        out_shape=[jax.ShapeDtypeStruct((ROWS, D_RNN), F32)] * 3
                  + [jax.ShapeDtypeStruct((ROWS, Q_DIM + 4 * KV_DIM), BF16)],
        scratch_shapes=[pltpu.VMEM((SUBLANES, D_RNN), F32)],
        compiler_params=_params(),
        name="l0_inproj",
    )(*([x2d] * nx), meta, norm_w.reshape(1, D_MODEL), w, wkv, conv_w, row(conv_b), wg, row(b_a), row(b_x), row(lam))


L1_IN_COLS = D_SSM + SSD_CONV_DIM + LANES
SSD_CONV_CHUNK = 4 * LANES


def _l1_inproj_kernel(h_ref, nw_ref, w_ref, cw_ref, cb_ref, dtb_ref, zs_ref, xc_ref, dt_ref, *xs_refs, tm, tpb):
    tile = pl.program_id(0) % tpb
    q = tm // SUBLANES
    spc = SSD_CONV_CHUNK // LANES
    n_chunks = SSD_CONV_DIM // SSD_CONV_CHUNK

    @pl.when(tile == 0)
    def _():
        for xs_ref in xs_refs:
            xs_ref[:, 0:SUBLANES, :] = jnp.zeros((spc, SUBLANES, LANES), F32)

    u_in = _rms(h_ref[...], nw_ref[...]).astype(BF16)
    sub = lax.broadcasted_iota(jnp.int32, (SUBLANES, LANES), 0)
    row0 = tile * tm + sub * q
    valid_p = [row0 + v >= PAD for v in range(q)]
    zw = D_SSM // n_chunks
    for c in range(n_chunks):
        xs_ref = xs_refs[c]
        x = jnp.dot(u_in, w_ref[:, D_SSM + c * SSD_CONV_CHUNK:D_SSM + (c + 1) * SSD_CONV_CHUNK],
                    preferred_element_type=F32)
        for j in range(spc):
            xs_ref[j, SUBLANES:SUBLANES + tm, :] = x[:, j * LANES:(j + 1) * LANES]
        zcols = slice(c * zw, (c + 1) * zw)
        zs_ref[:, zcols] = _silu(jnp.dot(u_in, w_ref[:, zcols], preferred_element_type=F32))
        for j in range(spc):
            lcols = slice(c * SSD_CONV_CHUNK + j * LANES, c * SSD_CONV_CHUNK + (j + 1) * LANES)
            wk = [jnp.broadcast_to(cw_ref[k:k + 1, lcols], (SUBLANES, LANES)) for k in range(CONV_C)]
            bias = jnp.broadcast_to(cb_ref[:, lcols], (SUBLANES, LANES))
            load = lambda start: xs_ref[j, pl.ds(start, SUBLANES, stride=q), :]
            window = [load(SUBLANES - k) for k in range(CONV_C - 1, 0, -1)]
            for v in range(q):
                window.append(load(SUBLANES + v))
                acc = bias
                for k in range(CONV_C):
                    acc = acc + wk[k] * window[k]
                window.pop(0)
                xc_ref[c * spc + j, pl.ds(v, SUBLANES, stride=q), :] = jnp.where(valid_p[v], _silu(acc), 0.0)
        for j in range(spc):
            xs_ref[j, 0:SUBLANES, :] = xs_ref[j, tm:tm + SUBLANES, :]
    dt_raw = jnp.dot(u_in, w_ref[:, D_SSM + SSD_CONV_DIM:], preferred_element_type=F32)
    dt_ref[...] = jnp.where(_row_valid(tile, tm), _softplus(dt_raw + dtb_ref[...]), 0.0)


def _l1_inproj(h, norm_w, w, conv_w, conv_b, dt_bias, tm):
    rowblk = lambda width: pl.BlockSpec((tm, width), lambda i: (i, 0))
    n_slabs = SSD_CONV_DIM // LANES
    return pl.pallas_call(
        functools.partial(_l1_inproj_kernel, tm=tm, tpb=LP // tm),
        grid=(ROWS // tm,),
        in_specs=[rowblk(D_MODEL),
                  _const_spec((1, D_MODEL)),
                  _const_spec((D_MODEL, L1_IN_COLS)),
                  _const_spec((CONV_C, SSD_CONV_DIM)),
                  _const_spec((1, SSD_CONV_DIM)),
                  _const_spec((1, LANES))],
        out_specs=[rowblk(D_SSM), pl.BlockSpec((n_slabs, tm, LANES), lambda i: (0, i, 0)), rowblk(LANES)],
        out_shape=[jax.ShapeDtypeStruct((ROWS, D_SSM), F32),
                   jax.ShapeDtypeStruct((n_slabs, ROWS, LANES), F32),
                   jax.ShapeDtypeStruct((ROWS, LANES), F32)],
        scratch_shapes=[pltpu.VMEM((SSD_CONV_CHUNK // LANES, SUBLANES + tm, LANES), F32)
                        for _ in range(SSD_CONV_DIM // SSD_CONV_CHUNK)],
        compiler_params=_params(),
        name="l1_inproj",
    )(h, norm_w.reshape(1, D_MODEL), w, conv_w, conv_b.reshape(1, SSD_CONV_DIM),
      jnp.pad(dt_bias, (0, LANES - SSD_HEADS)).reshape(1, LANES))


def _outproj_kernel(*refs, widths, from_tokens, tm, tpb):
    a_refs = refs[:len(widths)]
    w_ref, pw_ref = refs[len(widths):len(widths) + 2]
    res_refs = refs[len(widths) + 2:-1]
    o_ref = refs[-1]
    tile = pl.program_id(0) % tpb
    y = None
    r0 = 0
    for a_ref, width in zip(a_refs, widths):
        part = jnp.dot(a_ref[...], w_ref[r0:r0 + width, :], preferred_element_type=F32)
        y = part if y is None else y + part
        r0 += width
    h = _initial_stream_tile(tile, res_refs[:-1], res_refs[-1]) if from_tokens else res_refs[0][...]
    o_ref[...] = jnp.where(_row_valid(tile, tm), h + _rms(y, pw_ref[...]), 0.0)


def _outproj(acts, w, residual, post_w, tm, name):
    widths = tuple(a.shape[1] for a in acts)
    from_tokens = isinstance(residual, tuple)
    if from_tokens:
        nx = tm // BLOCK
        res_specs = [_token_block_spec(k, tm, 0) for k in range(nx)] + [_const_spec((N_META, D_MODEL))]
        res_args = [residual[0]] * nx + [residual[1]]
    else:
        res_specs = [pl.BlockSpec((tm, D_MODEL), lambda i: (i, 0))]
        res_args = [residual]
    return pl.pallas_call(
        functools.partial(_outproj_kernel, widths=widths, from_tokens=from_tokens, tm=tm, tpb=LP // tm),
        grid=(ROWS // tm,),
        in_specs=[pl.BlockSpec((tm, k), lambda i: (i, 0)) for k in widths]
                 + [_const_spec((sum(widths), D_MODEL)), _const_spec((1, D_MODEL))] + res_specs,
        out_specs=pl.BlockSpec((tm, D_MODEL), lambda i: (i, 0)),
        out_shape=jax.ShapeDtypeStruct((ROWS, D_MODEL), F32),
        compiler_params=_params(),
        name=name,
    )(*acts, w, post_w.reshape(1, D_MODEL), *res_args)


def _ffn_kernel(*refs, tm, tpb, tokens_only):
    nh = tm // BLOCK if tokens_only else 1
    h_refs, meta_rows_ref = refs[:nh], (refs[nh] if tokens_only else None)
    nw_ref, wup_ref, cw_ref, cb_ref, wdn_ref, pw_ref, o_ref, carry_ref, act_ref = refs[nh + int(tokens_only):]
    tile = pl.program_id(0) % tpb

    @pl.when(tile == 0)
    def _():
        if tokens_only:
            um = _rms(meta_rows_ref[...], nw_ref[...]).astype(BF16)
            xm = jnp.dot(um, wup_ref[...], preferred_element_type=F32)
            carry_ref[...] = xm[N_META - SUBLANES:]
        else:
            carry_ref[...] = jnp.zeros_like(carry_ref)

    h = jnp.concatenate([r[...] for r in h_refs], axis=0) if nh > 1 else h_refs[0][...]
    u = _rms(h, nw_ref[...]).astype(BF16)

    def conv_chunk(c0, scale):
        cols = slice(c0, c0 + FF_CHUNK)
        x = jnp.dot(u, wup_ref[:, cols], preferred_element_type=F32)
        carry = carry_ref[:, cols]
        x1 = _with_history(carry, x, 1)
        x2 = _with_history(carry, x, 2)
        carry_ref[:, cols] = x[tm - SUBLANES:]
        w = cw_ref[:, cols] * scale
        return cb_ref[:, cols] * scale + w[2:3] * x + w[1:2] * x1 + w[0:1] * x2

    for j in range(N_FF_CHUNKS):
        g = conv_chunk(j * FF_CHUNK, 1.0)
        half_v = conv_chunk(D_FF + j * FF_CHUNK, 0.5)
        act_ref[:, j * FF_CHUNK:(j + 1) * FF_CHUNK] = _gelu_tanh_times(g, half_v).astype(BF16)

    y = jnp.dot(act_ref[...], wdn_ref[...], preferred_element_type=F32)
    o = h + _rms(y, pw_ref[...])
    o_ref[...] = o if tokens_only else jnp.where(_row_valid(tile, tm), o, 0.0)


def _ffn(h, pre_w, w_up, conv_w, conv_b, w_down, post_w, tm, name, tokens_only=False):
    if tokens_only:
        tpb, n_rows = SEQ // tm, BATCH * SEQ
        h_specs = [pl.BlockSpec((BLOCK, D_MODEL),
                                lambda i, k=k: ((i // tpb) * NBLK + 1 + (i % tpb) * (tm // BLOCK) + k, 0))
                   for k in range(tm // BLOCK)]
        h_specs.append(pl.BlockSpec((N_META, D_MODEL), lambda i: ((i // tpb) * (LP // N_META) + PAD // N_META, 0)))
        h_args = [h] * (tm // BLOCK + 1)
    else:
        tpb, n_rows = LP // tm, ROWS
        h_specs = [pl.BlockSpec((tm, D_MODEL), lambda i: (i, 0))]
        h_args = [h]
    return pl.pallas_call(
        functools.partial(_ffn_kernel, tm=tm, tpb=tpb, tokens_only=tokens_only),
        grid=(n_rows // tm,),
        in_specs=h_specs + [_const_spec((1, D_MODEL)),
                            _const_spec((D_MODEL, 2 * D_FF)),
                            _const_spec((CONV_F, 2 * D_FF)),
                            _const_spec((1, 2 * D_FF)),
                            _const_spec((D_FF, D_MODEL)),
                            _const_spec((1, D_MODEL))],
        out_specs=pl.BlockSpec((tm, D_MODEL), lambda i: (i, 0)),
        out_shape=jax.ShapeDtypeStruct((n_rows, D_MODEL), F32),
        scratch_shapes=[pltpu.VMEM((SUBLANES, 2 * D_FF), F32),
                        pltpu.VMEM((tm, D_FF), BF16)],
        compiler_params=_params(),
        name=name,
    )(*h_args, pre_w.reshape(1, D_MODEL), w_up.astype(BF16), conv_w, conv_b.reshape(1, 2 * D_FF),
      w_down.astype(BF16), post_w.reshape(1, D_MODEL))


def _linear_scan(a, u):
    n = a.shape[0]
    d = 1
    while d < n:
        a_s = _shift_rows(a, d, 1.0)
        u_s = _shift_rows(u, d, 0.0)
        u = a * u_s + u
        a = a * a_s
        d *= 2
    return a, u


def _lru_kernel(gg_ref, a_ref, u_ref, o_ref, ch_ref, *, tr):
    tile = pl.program_id(0) % (LP // tr)

    @pl.when(tile == 0)
    def _():
        ch_ref[...] = jnp.zeros_like(ch_ref)

    for n in range(LRU_BLOCKS):
        cols = slice(n * LRU_BS, (n + 1) * LRU_BS)
        carry = ch_ref[SUBLANES - 1:SUBLANES, cols]
        for r0 in range(0, tr, BLOCK):
            rows = slice(r0, r0 + BLOCK)
            a_cum, hloc = _linear_scan(a_ref[rows, cols], u_ref[rows, cols])
            h = hloc + a_cum * carry
            carry = h[BLOCK - 1:BLOCK]
            o_ref[rows, cols] = (gg_ref[rows, cols] * h).astype(o_ref.dtype)
        ch_ref[:, cols] = jnp.broadcast_to(carry, (SUBLANES, LRU_BS))


def _lru(gg, a, u, tr):
    rowblk = pl.BlockSpec((tr, D_RNN), lambda i: (i, 0))
    return pl.pallas_call(
        functools.partial(_lru_kernel, tr=tr),
        grid=(ROWS // tr,),
        in_specs=[rowblk, rowblk, rowblk],
        out_specs=rowblk,
        out_shape=jax.ShapeDtypeStruct((ROWS, D_RNN), BF16),
        scratch_shapes=[pltpu.VMEM((SUBLANES, D_RNN), F32)],
        compiler_params=_params(),
        name="l0_lru",
    )(gg, a, u)


def _attn_kernel(sink_ref, q_ref, km_ref, kp_ref, kc_ref, vm_ref, vp_ref, vc_ref, o_ref):
    n = pl.program_id(0) % NBLK
    jrow = lax.broadcasted_iota(jnp.int32, (BLOCK, BLOCK), 0)
    icol = lax.broadcasted_iota(jnp.int32, (BLOCK, BLOCK), 1)
    from_prev = jrow > icol
    neg_band = jnp.where(from_prev,
                         jnp.where(n >= 2, -(BLOCK + icol - jrow).astype(F32), -jnp.inf),
                         jnp.where(n >= 1, -(icol - jrow).astype(F32), -jnp.inf))
    m_row = lax.broadcasted_iota(jnp.int32, (N_META, BLOCK), 0)
    i_meta = lax.broadcasted_iota(jnp.int32, (N_META, BLOCK), 1)
    d_meta = n * BLOCK + i_meta - PAD - m_row
    neg_meta = jnp.where(d_meta >= 0, -jnp.minimum(d_meta, WINDOW).astype(F32), -jnp.inf)

    lane = lax.broadcasted_iota(jnp.int32, (1, LANES), 1)
    left = lane < HEAD_DIM
    halves = lambda x: (jnp.where(left, x, 0.0).astype(BF16), jnp.where(left, 0.0, x).astype(BF16))
    pairs_per_group = Q_PER_KV // 2
    scores, v_stacks = [], []
    for g in range(N_KV_HEADS):
        gcols = slice(g * LANES, (g + 1) * LANES)
        k_all = jnp.concatenate([kp_ref[:, gcols], kc_ref[:, gcols], km_ref[PAD:, gcols]], axis=0)
        k_all = k_all * (HEAD_DIM ** -0.5)
        v_left, v_right = halves(jnp.concatenate([vp_ref[:, gcols], vc_ref[:, gcols], vm_ref[PAD:, gcols]], axis=0))
        v_stacks.append(jnp.concatenate([v_left, v_right], axis=0))
        for pair in range(pairs_per_group):
            pcols = slice((g * pairs_per_group + pair) * LANES, (g * pairs_per_group + pair + 1) * LANES)
            q_left, q_right = halves(q_ref[:, pcols])
            q_both = jnp.concatenate([q_left, q_right], axis=0)
            scores.append(lax.dot_general(k_all, q_both, (((1,), (1,)), ((), ())), preferred_element_type=F32))
    for g in range(N_KV_HEADS):
        v_both = v_stacks[g]
        for pair in range(pairs_per_group):
            pcols = slice((g * pairs_per_group + pair) * LANES, (g * pairs_per_group + pair + 1) * LANES)
            s_t = scores[g * pairs_per_group + pair]
            p_parts = []
            for half in range(2):
                head = g * Q_PER_KV + 2 * pair + half
                slope = 2.0 ** (-8.0 * (head + 1) / N_Q_HEADS)
                sink = sink_ref[head]
                cols = slice(half * BLOCK, (half + 1) * BLOCK)
                band = jnp.where(from_prev, s_t[:BLOCK, cols], s_t[BLOCK:2 * BLOCK, cols]) + slope * neg_band
                meta = s_t[2 * BLOCK:, cols] + slope * neg_meta
                mx = jnp.maximum(jnp.maximum(jnp.max(band, axis=0, keepdims=True),
                                             jnp.max(meta, axis=0, keepdims=True)), sink)
                e_band = jnp.exp(band - mx)
                e_meta = jnp.exp(meta - mx)
                denom = (jnp.sum(e_band, axis=0, keepdims=True) + jnp.sum(e_meta, axis=0, keepdims=True)
                         + jnp.exp(sink - mx))
                r = 1.0 / denom
                p_band = e_band * r
                p_parts += [jnp.where(from_prev, p_band, 0.0).astype(BF16),
                            jnp.where(from_prev, 0.0, p_band).astype(BF16),
                            (e_meta * r).astype(BF16)]
            p_t = jnp.concatenate(p_parts, axis=0)
            o_pair = lax.dot_general(p_t, v_both, (((0,), (0,)), ((), ())), preferred_element_type=F32)
            o_ref[:, pcols] = o_pair.astype(o_ref.dtype)


def _attention(qkv, sinks):
    kcol = Q_DIM // (2 * LANES)
    vcol = kcol + 1
    first = lambda i: (i // NBLK) * NBLK
    prev = lambda i: jnp.maximum(i - 1, 0)
    kv = lambda rowf, col: pl.BlockSpec((BLOCK, 2 * LANES), lambda i: (rowf(i), col))
    return pl.pallas_call(
        _attn_kernel,
        grid=(ROWS // BLOCK,),
        in_specs=[pl.BlockSpec(memory_space=pltpu.SMEM),
                  pl.BlockSpec((BLOCK, Q_DIM), lambda i: (i, 0)),
                  kv(first, kcol), kv(prev, kcol), kv(lambda i: i, kcol),
                  kv(first, vcol), kv(prev, vcol), kv(lambda i: i, vcol)],
        out_specs=pl.BlockSpec((BLOCK, Q_DIM), lambda i: (i, 0)),
        out_shape=jax.ShapeDtypeStruct((ROWS, Q_DIM), BF16),
        compiler_params=_params(),
        name="l0_attn",
    )(sinks, qkv, qkv, qkv, qkv, qkv, qkv, qkv)


def _ssd_kernel(zs_ref, xc_ref, dt_ref, alog_ref, dskip_ref, gn_ref, o_ref, state_ref):
    n = pl.program_id(0) % NBLK

    @pl.when(n == 0)
    def _():
        state_ref[...] = jnp.zeros_like(state_ref)

    dt = dt_ref[...]
    a = -jnp.exp(alog_ref[...])
    cs = dt * a
    d = 1
    while d < BLOCK:
        cs = cs + _shift_rows(cs, d, 0.0)
        d *= 2
    cs_last = cs[BLOCK - 1:BLOCK, :]
    dtw = dt * jnp.exp(cs_last - cs)
    chunk_decay = jnp.exp(cs_last)
    e_cs_lanes = jnp.exp(cs)
    cs_mlog_t = (cs - jnp.log(dt)).T
    dtw_t = dtw.T

    row = lax.broadcasted_iota(jnp.int32, (BLOCK, BLOCK), 0)
    col = lax.broadcasted_iota(jnp.int32, (BLOCK, BLOCK), 1)
    tril = row >= col
    lane = lax.broadcasted_iota(jnp.int32, (1, LANES), 1)
    left = lane < SSD_HEADDIM

    gw = SSD_HPG * SSD_HEADDIM
    b_ts, cbs, y_offs = [], [], []
    for g in range(SSD_GROUPS):
        b_g = xc_ref[D_SSM // LANES + g]
        c_g = xc_ref[(D_SSM + SSD_BC) // LANES + g].astype(BF16)
        b_t = b_g.T
        b_ts.append(b_t)
        cbs.append(jnp.dot(c_g, b_t.astype(BF16), preferred_element_type=F32))
        y_offs.append(jnp.dot(c_g, state_ref[g].astype(BF16), preferred_element_type=F32))
    pairs = [(g, pair) for g in range(SSD_GROUPS) for pair in range(SSD_HPG // 2)]
    pair_cols = lambda g, pair: slice(g * gw + pair * LANES, g * gw + (pair + 1) * LANES)
    pair_slab = lambda g, pair: g * (SSD_HPG // 2) + pair
    lhs_all, e_cs_all = [], []
    for g, pair in pairs:
        h0 = g * SSD_HPG + 2 * pair
        lhs_top, lhs_bot, e_cs = [], [], []
        for h in (h0, h0 + 1):
            cs_col = jnp.broadcast_to(cs[:, h:h + 1], (BLOCK, BLOCK))
            seg = cs_col - cs_mlog_t[h:h + 1, :]
            decay_dt = jnp.exp(jnp.where(tril, seg, -jnp.inf))
            lhs_top.append((cbs[g] * decay_dt).astype(BF16))
            lhs_bot.append((b_ts[g] * dtw_t[h:h + 1, :]).astype(BF16))
            e_cs.append(jnp.broadcast_to(e_cs_lanes[:, h:h + 1], (BLOCK, BLOCK)))
        lhs_all.append(jnp.concatenate([jnp.concatenate(lhs_top, axis=1), jnp.concatenate(lhs_bot, axis=1)], axis=0))
        e_cs_all.append(jnp.where(left, e_cs[0], e_cs[1]))
    res_all = []
    for (g, pair), lhs in zip(pairs, lhs_all):
        xp = xc_ref[pair_slab(g, pair)]
        x_bd = jnp.concatenate([jnp.where(left, xp, 0.0), jnp.where(left, 0.0, xp)], axis=0).astype(BF16)
        res_all.append(jnp.dot(lhs, x_bd, preferred_element_type=F32))
    for g in range(SSD_GROUPS):
        y_parts = []
        for pair in range(SSD_HPG // 2):
            h0 = g * SSD_HPG + 2 * pair
            idx = g * (SSD_HPG // 2) + pair
            res = res_all[idx]
            pcols = pair_cols(g, pair)
            scols = slice(pair * LANES, (pair + 1) * LANES)
            y_pair = res[:BLOCK] + y_offs[g][:, scols] * e_cs_all[idx]
            dec = jnp.where(left, chunk_decay[:, h0:h0 + 1], chunk_decay[:, h0 + 1:h0 + 2])
            state_ref[g, :, scols] = state_ref[g, :, scols] * dec + res[BLOCK:]
            y_parts.append(y_pair + dskip_ref[:, pcols] * xc_ref[pair_slab(g, pair)])
        gcols = slice(g * gw, (g + 1) * gw)
        y_g = jnp.concatenate(y_parts, axis=1) * zs_ref[:, gcols]
        y_g = y_g * lax.rsqrt(jnp.mean(y_g * y_g, axis=-1, keepdims=True) + EPS)
        o_ref[:, gcols] = (y_g * gn_ref[:, gcols]).astype(o_ref.dtype)


def _ssd(zs, xc, dt, a_log, d_skip, gate_norm):
    rowblk = lambda w: pl.BlockSpec((BLOCK, w), lambda i: (i, 0))
    gw = SSD_HPG * SSD_HEADDIM
    return pl.pallas_call(
        _ssd_kernel,
        grid=(ROWS // BLOCK,),
        in_specs=[rowblk(D_SSM), pl.BlockSpec((SSD_CONV_DIM // LANES, BLOCK, LANES), lambda i: (0, i, 0)), rowblk(LANES),
                  _const_spec((1, LANES)),
                  _const_spec((1, D_SSM)),
                  _const_spec((1, D_SSM))],
        out_specs=rowblk(D_SSM),
        out_shape=jax.ShapeDtypeStruct((ROWS, D_SSM), BF16),
        scratch_shapes=[pltpu.VMEM((SSD_GROUPS, SSD_STATE, gw), F32)],
        compiler_params=_params(),
        name="l1_ssd",
    )(zs, xc, dt, jnp.pad(a_log, (0, LANES - SSD_HEADS)).reshape(1, LANES),
      jnp.repeat(d_skip, SSD_HEADDIM).reshape(1, D_SSM), gate_norm.reshape(1, D_SSM))


def kernel(x, meta_tokens,
           l0_mix_pre_norm, l0_mix_post_norm, l0_w_in, l0_lru_conv_w, l0_lru_conv_b,
           l0_lru_w_a, l0_lru_b_a, l0_lru_w_x, l0_lru_b_x, l0_lru_lambda, l0_attn_sinks, l0_w_out,
           l0_ffn_pre_norm, l0_ffn_post_norm, l0_ffn_w_up, l0_ffn_conv_w, l0_ffn_conv_b, l0_ffn_w_down,
           l1_mix_pre_norm, l1_mix_post_norm, l1_w_in, l1_ssm_conv_w, l1_ssm_conv_b,
           l1_dt_bias, l1_a_log, l1_d_skip, l1_gate_norm, l1_w_out,
           l1_ffn_pre_norm, l1_ffn_post_norm, l1_ffn_w_up, l1_ffn_conv_w, l1_ffn_conv_b, l1_ffn_w_down):
    bsz = x.shape[0]
    x2d = x.reshape(bsz * SEQ, D_MODEL)
    meta = meta_tokens.astype(x.dtype)

    kq = 2 * D_RNN + Q_DIM
    dup = lambda w: jnp.concatenate([w[:, :HEAD_DIM]] * 2 + [w[:, HEAD_DIM:]] * 2, axis=1)
    w0 = l0_w_in[:, :kq].astype(BF16)
    wkv = jnp.concatenate([dup(l0_w_in[:, kq:kq + KV_DIM]), dup(l0_w_in[:, kq + KV_DIM:])], axis=1).astype(BF16)
    gg, a, u, qkv = _l0_inproj(x2d, meta, l0_mix_pre_norm, w0, wkv, l0_lru_conv_w, l0_lru_conv_b, l0_lru_w_a, l0_lru_b_a,
                               l0_lru_w_x, l0_lru_b_x, l0_lru_lambda, 640)
    y_a = _lru(gg, a, u, 5 * BLOCK)
    y_b = _attention(qkv, l0_attn_sinks)
    h = _outproj([y_a, y_b], l0_w_out.astype(BF16), (x2d, meta), l0_mix_post_norm, 640, "l0_outproj")
    h = _ffn(h, l0_ffn_pre_norm, l0_ffn_w_up, l0_ffn_conv_w, l0_ffn_conv_b, l0_ffn_w_down, l0_ffn_post_norm,
             640, "l0_ffn")

    w1 = jnp.pad(l1_w_in, ((0, 0), (0, LANES - SSD_HEADS))).astype(BF16)
    zs, xc, dt = _l1_inproj(h, l1_mix_pre_norm, w1, l1_ssm_conv_w, l1_ssm_conv_b, l1_dt_bias, 416)
    y = _ssd(zs, xc, dt, l1_a_log, l1_d_skip, l1_gate_norm)
    h = _outproj([y], l1_w_out.astype(BF16), h, l1_mix_post_norm, 640, "l1_outproj")
    out = _ffn(h, l1_ffn_pre_norm, l1_ffn_w_up, l1_ffn_conv_w, l1_ffn_conv_b, l1_ffn_w_down, l1_ffn_post_norm,
               4 * BLOCK, "l1_ffn", tokens_only=True)
    return out.reshape(bsz, SEQ, D_MODEL)
```
